```python
import jax, jax.numpy as jnp
from jax import lax
import numpy as np

D_MODEL = 4096
BATCH = 4
SEQ = 4096
DEPTH = 4

N_MIXERS = 2
N_CONV_LAYERS = (DEPTH + 1) // 2
N_MLSTM_LAYERS = DEPTH // 2
COND_RANK = 512
CONV_WIDTH = 31
MLSTM_HEADS = 8
MLSTM_DQK = D_MODEL // (2 * MLSTM_HEADS)
MLSTM_DV = D_MODEL // MLSTM_HEADS
MLSTM_CHUNK = 64
MLSTM_PROJ = MLSTM_HEADS * (2 * MLSTM_DQK + 2 * MLSTM_DV + 2)
D_FF = 2 * D_MODEL
N_EXPERTS = 8
TOP_K = 2
D_FF_EXPERT = D_MODEL // 2
EPS = 1e-6

kernel_name = "hybrid_conv_mlstm_moe_adaln_block"


def rms_norm(x, g):
    xf = x.astype(jnp.float32)
    y = xf * lax.rsqrt(jnp.mean(xf * xf, axis=-1, keepdims=True) + EPS)
    return (y * g.astype(jnp.float32)).astype(x.dtype)


def layer_norm(x, g, b):
    xf = x.astype(jnp.float32)
    mu = jnp.mean(xf, axis=-1, keepdims=True)
    var = jnp.mean(jnp.square(xf - mu), axis=-1, keepdims=True)
    return ((xf - mu) * lax.rsqrt(var + EPS) * g.astype(jnp.float32) + b.astype(jnp.float32)).astype(x.dtype)


def swiglu(t, w_gate, w_up, w_down):
    return (jax.nn.silu(t @ w_gate) * (t @ w_up)) @ w_down


def conformer_conv(h, w_in, b_in, w_dw, b_dw, ln_g, ln_b, w_out, b_out):
    u = h @ w_in + b_in
    val, gate = jnp.split(u, 2, axis=-1)
    u = val * jax.nn.sigmoid(gate)
    u = lax.conv_general_dilated(
        u, w_dw[:, None, :], window_strides=(1,), padding=[(CONV_WIDTH - 1, 0)],
        dimension_numbers=("NWC", "WIO", "NWC"), feature_group_count=D_MODEL) + b_dw
    u = jax.nn.silu(layer_norm(u, ln_g, ln_b))
    return u @ w_out + b_out


def mlstm_chunk_step(carry, xs):
    C, n, m = carry
    q, k, v, ig, lf = xs
    L = q.shape[2]
    causal = jnp.tril(jnp.ones((L, L), dtype=bool))
    b = jnp.cumsum(lf, axis=-1)
    a = b + m[..., None]
    dmat = jnp.where(causal, b[..., :, None] - b[..., None, :] + ig[..., None, :], -jnp.inf)
    m_s = jnp.maximum(a, jnp.max(dmat, axis=-1))
    w_inter = jnp.exp(a - m_s)
    s = jnp.einsum("bhsd,bhjd->bhsj", q, k) * jnp.exp(dmat - m_s[..., None])
    num = jnp.einsum("bhsj,bhjv->bhsv", s, v) + w_inter[..., None] * jnp.einsum("bhsd,bhdv->bhsv", q, C)
    den = jnp.sum(s, axis=-1) + w_inter * jnp.einsum("bhsd,bhd->bhs", q, n)
    h = num / jnp.maximum(jnp.abs(den), jnp.exp(-m_s))[..., None]
    b_end = b[..., -1]
    g = b_end[..., None] - b + ig
    m_new = jnp.maximum(b_end + m, jnp.max(g, axis=-1))
    decay = jnp.exp(b_end + m - m_new)
    wk = k * jnp.exp(g - m_new[..., None])[..., None]
    C = decay[..., None, None] * C + jnp.einsum("bhjd,bhjv->bhdv", wk, v)
    n = decay[..., None] * n + jnp.sum(wk, axis=-2)
    return (C, n, m_new), h


def mlstm_mixer(h, w_in, b_gates, norm_g, w_out):
    B, S, _ = h.shape
    H, DK, DV, L = MLSTM_HEADS, MLSTM_DQK, MLSTM_DV, MLSTM_CHUNK
    NC = S // L
    f32 = jnp.float32
    proj = h @ w_in
    q, k, v, o, gates = jnp.split(proj, [H * DK, 2 * H * DK, 2 * H * DK + H * DV, 2 * H * DK + 2 * H * DV], axis=-1)
    gates = gates.astype(f32) + b_gates.astype(f32)
    ig = gates[..., :H]
    lf = jax.nn.log_sigmoid(gates[..., H:])

    def to_chunks(t, d):
        return t.astype(f32).reshape(B, NC, L, H, d).transpose(1, 0, 3, 2, 4)

    def gate_chunks(t):
        return t.reshape(B, NC, L, H).transpose(1, 0, 3, 2)

    xs = (to_chunks(q, DK), to_chunks(k, DK) * (DK ** -0.5), to_chunks(v, DV), gate_chunks(ig), gate_chunks(lf))
    init = (jnp.zeros((B, H, DK, DV), f32), jnp.zeros((B, H, DK), f32), jnp.zeros((B, H), f32))
    _, hc = lax.scan(mlstm_chunk_step, init, xs)
    hs = hc.transpose(1, 0, 3, 2, 4).reshape(B, S, H, DV)
    hs = hs * lax.rsqrt(jnp.mean(hs * hs, axis=-1, keepdims=True) + EPS)
    hs = hs.reshape(B, S, H * DV) * norm_g.astype(f32)
    y = (hs * jax.nn.sigmoid(o.astype(f32))).astype(h.dtype)
    return y @ w_out


def moe_swiglu(h, w_router, b_router, w_gate, w_up, w_down):
    B, S, D = h.shape
    t = h.reshape(B * S, D)
    logits = (t @ w_router).astype(jnp.float32) + b_router.astype(jnp.float32)
    top_vals, top_idx = lax.top_k(logits, TOP_K)
    top_w = jax.nn.softmax(top_vals, axis=-1)
    combine = jnp.sum(jax.nn.one_hot(top_idx, N_EXPERTS, dtype=jnp.float32) * top_w[..., None], axis=1)
    combine = combine.astype(t.dtype)
    out = jnp.zeros_like(t)
    for e in range(N_EXPERTS):
        out = out + combine[:, e:e + 1] * swiglu(t, w_gate[e], w_up[e], w_down[e])
    return out.reshape(B, S, D)


def setup_inputs(seed: int = 0) -> dict:
    key = jax.random.key(seed)
    ks = list(jax.random.split(key, 40))
    D, R, E = D_MODEL, COND_RANK, N_EXPERTS
    NA, NB = N_CONV_LAYERS, N_MLSTM_LAYERS
    H = MLSTM_HEADS

    def nrm(shape, scale):
        return jax.random.normal(ks.pop(), shape, jnp.float32) * scale

    def gain(shape):
        return 1.0 + nrm(shape, 0.05)

    b_i = nrm((NB, H), 0.1)
    b_f = jnp.linspace(3.0, 6.0, H, dtype=jnp.float32)[None, :] + nrm((NB, H), 0.1)
    return {
        "x": nrm((BATCH, SEQ, D), 1.0),
        "c": nrm((BATCH, D), 1.0),
        "cond_w": nrm((D, R), D ** -0.5),
        "cond_b": nrm((R,), 0.02),
        "ada_w": nrm((DEPTH, R, 6 * D), 0.5 * R ** -0.5),
        "ada_b": nrm((DEPTH, 6 * D), 0.02),
        "mix_norm_g": gain((DEPTH, D)),
        "ffn_norm_g": gain((DEPTH, D)),
        "final_norm_g": gain((D,)),
        "conv_w_in": nrm((NA, D, 2 * D), D ** -0.5),
        "conv_b_in": nrm((NA, 2 * D), 0.02),
        "conv_w_dw": nrm((NA, CONV_WIDTH, D), CONV_WIDTH ** -0.5),
        "conv_b_dw": nrm((NA, D), 0.02),
        "conv_ln_g": gain((NA, D)),
        "conv_ln_b": nrm((NA, D), 0.02),
        "conv_w_out": nrm((NA, D, D), D ** -0.5),
        "conv_b_out": nrm((NA, D), 0.02),
        "mlstm_w_in": nrm((NB, D, MLSTM_PROJ), D ** -0.5),
        "mlstm_b_gates": jnp.concatenate([b_i, b_f], axis=-1),
        "mlstm_norm_g": gain((NB, H * MLSTM_DV)),
        "mlstm_w_out": nrm((NB, H * MLSTM_DV, D), (H * MLSTM_DV) ** -0.5),
        "ffn_w_gate": nrm((NA, D, D_FF), D ** -0.5),
        "ffn_w_up": nrm((NA, D, D_FF), D ** -0.5),
        "ffn_w_down": nrm((NA, D_FF, D), D_FF ** -0.5),
        "moe_w_router": nrm((NB, D, E), D ** -0.5),
        "moe_b_router": nrm((NB, E), 0.01),
        "moe_w_gate": nrm((NB, E, D, D_FF_EXPERT), D ** -0.5),
        "moe_w_up": nrm((NB, E, D, D_FF_EXPERT), D ** -0.5),
        "moe_w_down": nrm((NB, E, D_FF_EXPERT, D), D_FF_EXPERT ** -0.5),
    }


def reference(x, c, cond_w, cond_b, ada_w, ada_b, mix_norm_g, ffn_norm_g, final_norm_g,
              conv_w_in, conv_b_in, conv_w_dw, conv_b_dw, conv_ln_g, conv_ln_b, conv_w_out, conv_b_out,
              mlstm_w_in, mlstm_b_gates, mlstm_norm_g, mlstm_w_out,
              ffn_w_gate, ffn_w_up, ffn_w_down,
              moe_w_router, moe_b_router, moe_w_gate, moe_w_up, moe_w_down):
    e = jax.nn.silu(c @ cond_w + cond_b)
    for i in range(DEPTH):
        j = i // N_MIXERS
        mod = (e @ ada_w[i] + ada_b[i])[:, None, :]
        shift1, scale1, gate1, shift2, scale2, gate2 = jnp.split(mod, 6, axis=-1)
        h = rms_norm(x, mix_norm_g[i]) * (1.0 + scale1) + shift1
        if i % N_MIXERS == 0:
            y = conformer_conv(h, conv_w_in[j], conv_b_in[j], conv_w_dw[j], conv_b_dw[j],
                               conv_ln_g[j], conv_ln_b[j], conv_w_out[j], conv_b_out[j])
        else:
            y = mlstm_mixer(h, mlstm_w_in[j], mlstm_b_gates[j], mlstm_norm_g[j], mlstm_w_out[j])
        x = x + gate1 * y
        h = rms_norm(x, ffn_norm_g[i]) * (1.0 + scale2) + shift2
        if i % 2 == 0:
            y = swiglu(h, ffn_w_gate[j], ffn_w_up[j], ffn_w_down[j])
        else:
            y = moe_swiglu(h, moe_w_router[j], moe_b_router[j], moe_w_gate[j], moe_w_up[j], moe_w_down[j])
        x = x + gate2 * y
    return rms_norm(x, final_norm_g)
```

```python
import functools

import jax
import jax.numpy as jnp
from jax import lax
from jax.experimental import pallas as pl
from jax.experimental.pallas import tpu as pltpu

EPS = 1e-6
F32 = jnp.float32
BF16 = jnp.bfloat16

LANES = 128
SUBLANES = 8
VMEM_LIMIT_BYTES = 56 * 1024 * 1024

ROW_TILE = 1024
COL_TILE_SINGLE = 1024
COL_TILE_FUSED = 512
K_TILE_SPLIT = 2048
K_SINGLE_MAX = 4096
NORM_ROWS = 256
CONV_ROWS = 256
CONV_HALO = 32
MLSTM_CHUNK = 256


def _tile(dim, preferred, align):
    if dim <= preferred:
        return dim
    t = preferred - preferred % align
    while t > align and dim % t:
        t -= align
    assert dim % t == 0, (dim, preferred, align)
    return t


def _params(*sem):
    return pltpu.CompilerParams(dimension_semantics=sem, vmem_limit_bytes=VMEM_LIMIT_BYTES)


def _sigmoid(z):
    return 1.0 / (1.0 + jnp.exp(-z))


def _dot(a, b):
    return jnp.dot(a, b, preferred_element_type=F32)


def _cond_embed_kernel(c_ref, w_ref, b_ref, o_ref):
    z = jnp.dot(c_ref[...], w_ref[...], preferred_element_type=F32,
                precision=lax.Precision.HIGHEST) + b_ref[...]
    o_ref[...] = z * _sigmoid(z)


def _cond_embed(c_pad, cond_w, cond_b):
    rows, r = c_pad.shape[0], cond_w.shape[1]
    return pl.pallas_call(
        _cond_embed_kernel,
        out_shape=jax.ShapeDtypeStruct((rows, r), F32),
        compiler_params=pltpu.CompilerParams(vmem_limit_bytes=VMEM_LIMIT_BYTES),
        name="cond_embed",
    )(c_pad, cond_w, cond_b.reshape(1, r))


def _cond_mod_kernel(e_ref, w_ref, b_ref, o_ref):
    o_ref[0] = jnp.dot(e_ref[...], w_ref[0], preferred_element_type=F32,
                       precision=lax.Precision.HIGHEST) + b_ref[0]


def _cond_mod(e, ada_w, ada_b):
    depth, r, width = ada_w.shape
    rows = e.shape[0]
    bn = _tile(width, 2048, LANES)
    return pl.pallas_call(
        _cond_mod_kernel,
        grid=(depth, width // bn),
        in_specs=[
            pl.BlockSpec((rows, r), lambda i, j: (0, 0)),
            pl.BlockSpec((1, r, bn), lambda i, j: (i, 0, j)),
            pl.BlockSpec((1, 1, bn), lambda i, j: (i, 0, j)),
        ],
        out_specs=pl.BlockSpec((1, rows, bn), lambda i, j: (i, 0, j)),
        out_shape=jax.ShapeDtypeStruct((depth, rows, width), F32),
        compiler_params=_params("parallel", "parallel"),
        name="cond_mod",
    )(e, ada_w, ada_b.reshape(depth, 1, width))


def _norm_kernel(x_ref, g_ref, *rest, modulate):
    if modulate:
        sc_ref, sh_ref, o_ref = rest
    else:
        (o_ref,) = rest
    x = x_ref[...]
    y = x * lax.rsqrt(jnp.mean(x * x, axis=-1, keepdims=True) + EPS) * g_ref[...]
    if modulate:
        y = y * (1.0 + sc_ref[0]) + sh_ref[0]
    o_ref[...] = y.astype(o_ref.dtype)


def _norm(x2, g, seq, scale=None, shift=None, out_dtype=BF16):
    n, d = x2.shape
    tm = _tile(seq, NORM_ROWS, SUBLANES)
    modulate = scale is not None
    in_specs = [pl.BlockSpec((tm, d), lambda i: (i, 0)),
                pl.BlockSpec((1, d), lambda i: (0, 0))]
    args = [x2, g.reshape(1, d)]
    if modulate:
        per_batch = pl.BlockSpec((1, 1, d), lambda i: (i * tm // seq, 0, 0))
        in_specs += [per_batch, per_batch]
        args += [scale, shift]
    return pl.pallas_call(
        functools.partial(_norm_kernel, modulate=modulate),
        grid=(n // tm,),
        in_specs=in_specs,
        out_specs=pl.BlockSpec((tm, d), lambda i: (i, 0)),
        out_shape=jax.ShapeDtypeStruct((n, d), out_dtype),
        compiler_params=_params("parallel"),
        name="norm_mod" if modulate else "rms_norm",
    )(*args)


def _mm_gated_kernel(a_ref, w1_ref, w2_ref, *rest, act, has_bias, has_scale):
    rest = list(rest)
    o_ref = rest.pop()
    a = a_ref[...]
    z1 = _dot(a, w1_ref[0])
    z2 = _dot(a, w2_ref[0])
    if has_bias:
        z1 = z1 + rest[0][...]
        z2 = z2 + rest[1][...]
    if act == "glu":
        y = z1 * _sigmoid(z2)
    else:
        y = (z1 * _sigmoid(z1)) * z2
    if has_scale:
        y = y * rest[-1][0]
    o_ref[...] = y.astype(o_ref.dtype)


def _mm_gated(a, w1, w1_map, w2, w2_map, n_out, *, act, out_dtype, group_cols=None,
              biases=None, scale=None, scale_map=None):
    m, k = a.shape
    bm = _tile(m, ROW_TILE, SUBLANES)
    bn = _tile(n_out if group_cols is None else group_cols, COL_TILE_FUSED, LANES)
    in_specs = [
        pl.BlockSpec((bm, k), lambda i, j: (i, 0)),
        pl.BlockSpec((1, k, bn), lambda i, j: w1_map(j, bn)),
        pl.BlockSpec((1, k, bn), lambda i, j: w2_map(j, bn)),
    ]
    args = [a, w1, w2]
    if biases is not None:
        (b1, b1_map), (b2, b2_map) = biases
        in_specs += [pl.BlockSpec((1, bn), lambda i, j: b1_map(j, bn)),
                     pl.BlockSpec((1, bn), lambda i, j: b2_map(j, bn))]
        args += [b1, b2]
    if scale is not None:
        in_specs.append(pl.BlockSpec((1, bm, 1), lambda i, j: scale_map(i, j, bn)))
        args.append(scale)
    return pl.pallas_call(
        functools.partial(_mm_gated_kernel, act=act, has_bias=biases is not None,
                          has_scale=scale is not None),
        grid=(m // bm, n_out // bn),
        in_specs=in_specs,
        out_specs=pl.BlockSpec((bm, bn), lambda i, j: (i, j)),
        out_shape=jax.ShapeDtypeStruct((m, n_out), out_dtype),
        compiler_params=_params("parallel", "parallel"),
        name="mm_" + act,
    )(*args)


def _mm_kernel(a_ref, w_ref, *rest, nk, has_bias, has_res):
    rest = list(rest)
    acc_ref = rest.pop() if nk > 1 else None
    o_ref = rest.pop()
    part = _dot(a_ref[...], w_ref[...])

    def finish(acc):
        idx = 0
        if has_bias:
            acc = acc + rest[idx][...]
            idx += 1
        if has_res:
            acc = rest[idx][...] + rest[idx + 1][0] * acc
        o_ref[...] = acc.astype(o_ref.dtype)

    if nk == 1:
        finish(part)
    else:
        kk = pl.program_id(2)

        @pl.when(kk == 0)
        def _():
            acc_ref[...] = part

        @pl.when(kk > 0)
        def _():
            acc_ref[...] += part

        @pl.when(kk == nk - 1)
        def _():
            finish(acc_ref[...])


def _mm(a, w, *, out_dtype, bias=None, res=None, gate=None, seq=None):
    m, k = a.shape
    n_out = w.shape[1]
    bm = _tile(m if seq is None else seq, ROW_TILE, SUBLANES)
    bn = _tile(n_out, COL_TILE_SINGLE if res is None else COL_TILE_FUSED, LANES)
    bk = k if k <= K_SINGLE_MAX else _tile(k, K_TILE_SPLIT, LANES)
    nk = k // bk
    in_specs = [pl.BlockSpec((bm, bk), lambda i, j, kk: (i, kk)),
                pl.BlockSpec((bk, bn), lambda i, j, kk: (kk, j))]
    args = [a, w]
    if bias is not None:
        in_specs.append(pl.BlockSpec((1, bn), lambda i, j, kk: (0, j)))
        args.append(bias.reshape(1, n_out))
    if res is not None:
        in_specs += [pl.BlockSpec((bm, bn), lambda i, j, kk: (i, j)),
                     pl.BlockSpec((1, 1, bn), lambda i, j, kk: (i * bm // seq, 0, j))]
        args += [res, gate]
    return pl.pallas_call(
        functools.partial(_mm_kernel, nk=nk, has_bias=bias is not None, has_res=res is not None),
        grid=(m // bm, n_out // bn, nk),
        in_specs=in_specs,
        out_specs=pl.BlockSpec((bm, bn), lambda i, j, kk: (i, j)),
        out_shape=jax.ShapeDtypeStruct((m, n_out), out_dtype),
        scratch_shapes=[pltpu.VMEM((bm, bn), F32)] if nk > 1 else [],
        compiler_params=_params("parallel", "parallel", "arbitrary"),
        name="mm_res" if res is not None else "mm",
    )(*args)


def _conv_kernel(u_ref, halo_ref, w_ref, bdw_ref, g_ref, b_ref, o_ref, buf_ref, y_ref, *, ts, kw, lc):
    d = y_ref.shape[1]
    s = pl.program_id(1)

    @pl.when(s == 0)
    def _():
        buf_ref[0:CONV_HALO, :] = jnp.zeros((CONV_HALO, d), F32)

    @pl.when(s > 0)
    def _():
        buf_ref[0:CONV_HALO, :] = halo_ref[0]

    buf_ref[CONV_HALO:, :] = u_ref[0]
    base = CONV_HALO - (kw - 1)

    def lane_chunk(c, carry):
        l0 = pl.multiple_of(c * lc, lc)
        lanes = pl.ds(l0, lc)
        taps = [jnp.broadcast_to(w_ref[k:k + 1, lanes], (SUBLANES, lc)) for k in range(kw)]
        bias = jnp.broadcast_to(bdw_ref[:, lanes], (SUBLANES, lc))
        for r0 in range(0, ts, SUBLANES):
            acc = bias
            for k in range(kw):
                acc = acc + taps[k] * buf_ref[pl.ds(r0 + base + k, SUBLANES), lanes]
            y_ref[pl.ds(r0, SUBLANES), lanes] = acc
        return carry

    lax.fori_loop(0, d // lc, lane_chunk, 0)

    y = y_ref[...]
    mu = jnp.mean(y, axis=-1, keepdims=True)
    yc = y - mu
    var = jnp.mean(yc * yc, axis=-1, keepdims=True)
    z = yc * lax.rsqrt(var + EPS) * g_ref[...] + b_ref[...]
    o_ref[0] = (z * _sigmoid(z)).astype(o_ref.dtype)


def _conv_ln_silu(u3, w_dw, b_dw, ln_g, ln_b):
    b, s, d = u3.shape
    kw = w_dw.shape[0]
    assert kw - 1 <= CONV_HALO
    ts = _tile(s, CONV_ROWS, CONV_HALO)
    halo_blocks = ts // CONV_HALO
    vec = pl.BlockSpec((1, d), lambda bi, si: (0, 0))
    return pl.pallas_call(
        functools.partial(_conv_kernel, ts=ts, kw=kw, lc=LANES),
        grid=(b, s // ts),
        in_specs=[
            pl.BlockSpec((1, ts, d), lambda bi, si: (bi, si, 0)),
            pl.BlockSpec((1, CONV_HALO, d), lambda bi, si: (bi, jnp.maximum(si * halo_blocks - 1, 0), 0)),
            pl.BlockSpec((kw, d), lambda bi, si: (0, 0)),
            vec, vec, vec,
        ],
        out_specs=pl.BlockSpec((1, ts, d), lambda bi, si: (bi, si, 0)),
        out_shape=jax.ShapeDtypeStruct((b, s, d), BF16),
        scratch_shapes=[pltpu.VMEM((ts + CONV_HALO, d), F32), pltpu.VMEM((ts, d), F32)],
        compiler_params=_params("parallel", "arbitrary"),
        name="dwconv_ln_silu",
    )(u3, u3, w_dw, b_dw.reshape(1, d), ln_g.reshape(1, d), ln_b.reshape(1, d))


def _gates_kernel(a_ref, w_ref, b_ref, o_ref, *, heads):
    z = _dot(a_ref[...], w_ref[...]) + b_ref[...]
    lane = lax.broadcasted_iota(jnp.int32, z.shape, 1)
    log_sig = -(jnp.maximum(-z, 0.0) + jnp.log1p(jnp.exp(-jnp.abs(z))))
    o_ref[...] = jnp.where((lane >= heads) & (lane < 2 * heads), log_sig, z)


def _mlstm_gates(h, w_pad, b_pad, heads):
    m, k = h.shape
    bm = _tile(m, ROW_TILE, SUBLANES)
    return pl.pallas_call(
        functools.partial(_gates_kernel, heads=heads),
        grid=(m // bm,),
        in_specs=[pl.BlockSpec((bm, k), lambda i: (i, 0)),
                  pl.BlockSpec((k, LANES), lambda i: (0, 0)),
                  pl.BlockSpec((1, LANES), lambda i: (0, 0))],
        out_specs=pl.BlockSpec((bm, LANES), lambda i: (i, 0)),
        out_shape=jax.ShapeDtypeStruct((m, LANES), F32),
        compiler_params=_params("parallel"),
        name="mlstm_gates",
    )(h, w_pad, b_pad)


def _mlstm_kernel(q_ref, k_ref, v_ref, o_ref, igr_ref, lfr_ref, igc_ref, lfc_ref, ng_ref,
                  out_ref, c_ref, n_ref, m_ref, *, chunk, dk):
    @pl.when(pl.program_id(2) == 0)
    def _():
        c_ref[...] = jnp.zeros_like(c_ref)
        n_ref[...] = jnp.zeros_like(n_ref)
        m_ref[...] = jnp.zeros_like(m_ref)

    q = q_ref[...]
    k = k_ref[...]
    v = v_ref[...]
    ig_r, lf_r = igr_ref[...], lfr_ref[...]
    ig_c, lf_c = igc_ref[...], lfc_ref[...]
    row = lax.broadcasted_iota(jnp.int32, (chunk, chunk), 0)
    col = lax.broadcasted_iota(jnp.int32, (chunk, chunk), 1)
    causal = col <= row
    b_c = jnp.sum(jnp.where(causal, lf_r, 0.0), axis=1, keepdims=True)
    b_r = jnp.sum(jnp.where(row <= col, lf_c, 0.0), axis=0, keepdims=True)
    b_end = jnp.sum(lf_r, axis=1, keepdims=True)
    m_prev = m_ref[...]
    a_c = b_c + m_prev
    dmat = jnp.where(causal, b_c - b_r + ig_r, -jnp.inf)
    m_s = jnp.maximum(a_c, jnp.max(dmat, axis=1, keepdims=True))
    w_inter = jnp.exp(a_c - m_s)
    scale = dk ** -0.5
    p = jnp.exp(dmat - m_s) * scale
    s = lax.dot_general(q, k, (((1,), (1,)), ((), ())), preferred_element_type=F32) * p
    c_state = c_ref[...]
    n_state = n_ref[...]
    num = _dot(s.astype(BF16), v) + w_inter * _dot(q, c_state.astype(BF16))
    den = (jnp.sum(s, axis=1, keepdims=True)
           + w_inter * jnp.sum(q.astype(F32) * n_state, axis=1, keepdims=True))
    h = num / jnp.maximum(jnp.abs(den), jnp.exp(-m_s))
    hn = h * lax.rsqrt(jnp.mean(h * h, axis=1, keepdims=True) + EPS) * ng_ref[...]
    out_ref[...] = (hn * _sigmoid(o_ref[...].astype(F32))).astype(out_ref.dtype)

    g_c = b_end - b_c + ig_c
    m_new = jnp.maximum(b_end + m_prev, jnp.max(g_c, axis=0, keepdims=True))
    decay = jnp.exp(b_end + m_prev - m_new)
    wk = k.astype(F32) * (jnp.exp(g_c - m_new) * scale)
    c_ref[...] = decay * c_state + lax.dot_general(
        wk.astype(BF16), v, (((0,), (0,)), ((), ())), preferred_element_type=F32)
    n_ref[...] = decay * n_state + jnp.sum(wk, axis=0, keepdims=True)
    m_ref[...] = m_new


def _mlstm_cell(proj, gates_rows, gates_cols, norm_g, batch, seq, heads, dk, dv):
    n = proj.shape[0]
    chunk = _tile(seq, MLSTM_CHUNK, LANES)
    nc = seq // chunk
    v_off = 2 * heads * dk // dv
    row_blk = lambda b, h, c: b * nc + c
    g_row = lambda off: pl.BlockSpec((None, None, None, 1, chunk), lambda b, h, c: (b, off + h, c, 0, 0))
    g_col = lambda off: pl.BlockSpec((None, None, None, chunk, 1), lambda b, h, c: (b, off + h, c, 0, 0))
    return pl.pallas_call(
        functools.partial(_mlstm_kernel, chunk=chunk, dk=dk),
        grid=(batch, heads, nc),
        in_specs=[
            pl.BlockSpec((chunk, dk), lambda b, h, c: (row_blk(b, h, c), h)),
            pl.BlockSpec((chunk, dk), lambda b, h, c: (row_blk(b, h, c), heads + h)),
            pl.BlockSpec((chunk, dv), lambda b, h, c: (row_blk(b, h, c), v_off + h)),
            pl.BlockSpec((chunk, dv), lambda b, h, c: (row_blk(b, h, c), v_off + heads + h)),
            g_row(0), g_row(heads), g_col(0), g_col(heads),
            pl.BlockSpec((1, dv), lambda b, h, c: (0, h)),
        ],
        out_specs=pl.BlockSpec((chunk, dv), lambda b, h, c: (row_blk(b, h, c), h)),
        out_shape=jax.ShapeDtypeStruct((n, heads * dv), BF16),
        scratch_shapes=[pltpu.VMEM((dk, dv), F32), pltpu.VMEM((1, dk), F32), pltpu.VMEM((1, 1), F32)],
        compiler_params=_params("parallel", "parallel", "arbitrary"),
        name="mlstm_cell",
    )(proj, proj, proj, proj, gates_rows, gates_rows, gates_cols, gates_cols, norm_g.reshape(1, heads * dv))


def _router_kernel(a_ref, w_ref, b_ref, o_ref, *, experts):
    logits = _dot(a_ref[...], w_ref[...]) + b_ref[...]
    lane = lax.broadcasted_iota(jnp.int32, logits.shape, 1)
    neg = -jnp.inf
    logits = jnp.where(lane < experts, logits, neg)
    v1 = jnp.max(logits, axis=1, keepdims=True)
    i1 = jnp.min(jnp.where(logits == v1, lane, LANES), axis=1, keepdims=True)
    rest = jnp.where(lane == i1, neg, logits)
    v2 = jnp.max(rest, axis=1, keepdims=True)
    i2 = jnp.min(jnp.where(rest == v2, lane, LANES), axis=1, keepdims=True)
    e2 = jnp.exp(v2 - v1)
    w1 = 1.0 / (1.0 + e2)
    w2 = e2 / (1.0 + e2)
    o_ref[...] = jnp.where(lane == i1, w1, 0.0) + jnp.where(lane == i2, w2, 0.0)


def _router(h, w_pad, b_pad, experts):
    m, k = h.shape
    bm = _tile(m, ROW_TILE, SUBLANES)
    return pl.pallas_call(
        functools.partial(_router_kernel, experts=experts),
        grid=(m // bm,),
        in_specs=[pl.BlockSpec((bm, k), lambda i: (i, 0)),
                  pl.BlockSpec((k, LANES), lambda i: (0, 0)),
                  pl.BlockSpec((1, LANES), lambda i: (0, 0))],
        out_specs=pl.BlockSpec((bm, LANES), lambda i: (i, 0)),
        out_shape=jax.ShapeDtypeStruct((m, LANES), F32),
        compiler_params=_params("parallel"),
        name="moe_router",
    )(h, w_pad, b_pad)


def _pad_cols(w, width):
    return jnp.pad(w, ((0, 0), (0, width - w.shape[1])))


def kernel(x, c, cond_w, cond_b, ada_w, ada_b, mix_norm_g, ffn_norm_g, final_norm_g, conv_w_in, conv_b_in, conv_w_dw, conv_b_dw, conv_ln_g, conv_ln_b, conv_w_out, conv_b_out, mlstm_w_in, mlstm_b_gates, mlstm_norm_g, mlstm_w_out, ffn_w_gate, ffn_w_up, ffn_w_down, moe_w_router, moe_b_router, moe_w_gate, moe_w_up, moe_w_down):
    batch, seq, d = x.shape
    n = batch * seq
    depth = ada_w.shape[0]
    heads = mlstm_b_gates.shape[1] // 2
    dv = mlstm_norm_g.shape[1] // heads
    dk = (mlstm_w_in.shape[2] - 2 * heads * dv - 2 * heads) // (2 * heads)
    experts = moe_w_router.shape[2]
    d_exp = moe_w_gate.shape[3]
    assert dk % LANES == 0 and dv % LANES == 0 and 2 * heads <= LANES and experts <= LANES

    c_pad = jnp.pad(c, ((0, SUBLANES - batch % SUBLANES if batch % SUBLANES else 0), (0, 0)))
    e = _cond_embed(c_pad, cond_w, cond_b)
    mods = _cond_mod(e, ada_w, ada_b)[:, :batch].reshape(depth, batch, 6, 1, d)

    xf = x.reshape(n, d)
    for i in range(depth):
        j = i // 2
        shift1, scale1, gate1, shift2, scale2, gate2 = [mods[i, :, t] for t in range(6)]

        h = _norm(xf, mix_norm_g[i], seq, scale1, shift1)
        if i % 2 == 0:
            w_in = conv_w_in[j].astype(BF16).reshape(1, d, 2 * d)
            b_in = conv_b_in[j].reshape(1, 2 * d)
            u = _mm_gated(
                h, w_in, lambda jb, bn: (0, 0, jb), w_in, lambda jb, bn: (0, 0, jb + d // bn), d,
                act="glu", out_dtype=F32,
                biases=((b_in, lambda jb, bn: (0, jb)), (b_in, lambda jb, bn: (0, jb + d // bn))))
            v = _conv_ln_silu(u.reshape(batch, seq, d), conv_w_dw[j], conv_b_dw[j], conv_ln_g[j], conv_ln_b[j])
            xf = _mm(v.reshape(n, d), conv_w_out[j].astype(BF16), out_dtype=F32,
                     bias=conv_b_out[j], res=xf, gate=gate1, seq=seq)
        else:
            n_main = 2 * heads * (dk + dv)
            w_main = mlstm_w_in[j][:, :n_main].astype(BF16)
            w_gates = _pad_cols(mlstm_w_in[j][:, n_main:], LANES).astype(BF16)
            b_gates = _pad_cols(mlstm_b_gates[j].reshape(1, 2 * heads), LANES)
            proj = _mm(h, w_main, out_dtype=BF16)
            gates = _mlstm_gates(h, w_gates, b_gates, heads)[:, :2 * heads]
            chunk = _tile(seq, MLSTM_CHUNK, LANES)
            gates_t = gates.reshape(batch, seq // chunk, chunk, 2 * heads).transpose(0, 3, 1, 2)
            y = _mlstm_cell(proj, gates_t[:, :, :, None, :], gates_t[:, :, :, :, None],
                            mlstm_norm_g[j], batch, seq, heads, dk, dv)
            xf = _mm(y, mlstm_w_out[j].astype(BF16), out_dtype=F32, res=xf, gate=gate1, seq=seq)

        h = _norm(xf, ffn_norm_g[i], seq, scale2, shift2)
        if i % 2 == 0:
            d_ff = ffn_w_gate.shape[2]
            wg = ffn_w_gate[j].astype(BF16).reshape(1, d, d_ff)
            wu = ffn_w_up[j].astype(BF16).reshape(1, d, d_ff)
            col = lambda jb, bn: (0, 0, jb)
            g = _mm_gated(h, wg, col, wu, col, d_ff, act="swiglu", out_dtype=BF16)
            xf = _mm(g, ffn_w_down[j].astype(BF16), out_dtype=F32, res=xf, gate=gate2, seq=seq)
        else:
            w_r = _pad_cols(moe_w_router[j], LANES).astype(BF16)
            b_r = _pad_cols(moe_b_router[j].reshape(1, experts), LANES)
            combine = _router(h, w_r, b_r, experts)
            comb_t = combine[:, :experts].T.reshape(experts, n, 1)
            wg = moe_w_gate[j].astype(BF16)
            wu = moe_w_up[j].astype(BF16)
            per_exp = lambda jb, bn: (jb * bn // d_exp, 0, jb % (d_exp // bn))
            g = _mm_gated(h, wg, per_exp, wu, per_exp, experts * d_exp, act="swiglu", out_dtype=BF16,
                          group_cols=d_exp, scale=comb_t, scale_map=lambda ib, jb, bn: (jb * bn // d_exp, ib, 0))
            w_down = moe_w_down[j].astype(BF16).reshape(experts * d_exp, d)
            xf = _mm(g, w_down, out_dtype=F32, res=xf, gate=gate2, seq=seq)

    out = _norm(xf, final_norm_g, seq, out_dtype=F32)
    return out.reshape(batch, seq, d)
```

```python
import functools

import jax
import jax.numpy as jnp
from jax import lax
from jax.experimental import pallas as pl
from jax.experimental.pallas import tpu as pltpu

EPS = 1e-6
F32 = jnp.float32
BF16 = jnp.bfloat16

LANES = 128
SUBLANES = 8
VMEM_LIMIT_BYTES = 56 * 1024 * 1024

ROW_TILE = 1024
COL_TILE_SINGLE = 1024
COL_TILE_FUSED = 512
K_TILE_SPLIT = 2048
K_SINGLE_MAX = 4096
NORM_ROWS = 256
CONV_ROWS = 256
CONV_HALO = 32
CONV_GROUP = 64
MLSTM_CHUNK = 256
MOE_ROWS = 512
GATHER_ROWS = 256
ZERO_ROWS = 64


def _tile(dim, preferred, align):
    if dim <= preferred:
        return dim
    t = preferred - preferred % align
    while t > align and dim % t:
        t -= align
    assert dim % t == 0, (dim, preferred, align)
    return t


def _params(*sem):
    return pltpu.CompilerParams(dimension_semantics=sem, vmem_limit_bytes=VMEM_LIMIT_BYTES)


def _sigmoid(z):
    return 1.0 / (1.0 + jnp.exp(-z))


def _dot(a, b):
    return jnp.dot(a, b, preferred_element_type=F32)


def _cond_embed_kernel(c_ref, w_ref, b_ref, o_ref):
    z = jnp.dot(c_ref[...], w_ref[...], preferred_element_type=F32,
                precision=lax.Precision.HIGHEST) + b_ref[...]
    o_ref[...] = z * _sigmoid(z)


def _cond_embed(c_pad, cond_w, cond_b):
    rows, r = c_pad.shape[0], cond_w.shape[1]
    return pl.pallas_call(
        _cond_embed_kernel,
        out_shape=jax.ShapeDtypeStruct((rows, r), F32),
        compiler_params=pltpu.CompilerParams(vmem_limit_bytes=VMEM_LIMIT_BYTES),
        name="cond_embed",
    )(c_pad, cond_w, cond_b.reshape(1, r))


def _cond_mod_kernel(e_ref, w_ref, b_ref, o_ref):
    o_ref[0] = jnp.dot(e_ref[...], w_ref[0], preferred_element_type=F32,
                       precision=lax.Precision.HIGHEST) + b_ref[0]


def _cond_mod(e, ada_w, ada_b):
    depth, r, width = ada_w.shape
    rows = e.shape[0]
    bn = _tile(width, 2048, LANES)
    return pl.pallas_call(
        _cond_mod_kernel,
        grid=(depth, width // bn),
        in_specs=[
            pl.BlockSpec((rows, r), lambda i, j: (0, 0)),
            pl.BlockSpec((1, r, bn), lambda i, j: (i, 0, j)),
            pl.BlockSpec((1, 1, bn), lambda i, j: (i, 0, j)),
        ],
        out_specs=pl.BlockSpec((1, rows, bn), lambda i, j: (i, 0, j)),
        out_shape=jax.ShapeDtypeStruct((depth, rows, width), F32),
        compiler_params=_params("parallel", "parallel"),
        name="cond_mod",
    )(e, ada_w, ada_b.reshape(depth, 1, width))


def _norm_kernel(x_ref, g_ref, *rest, modulate):
    if modulate:
        sc_ref, sh_ref, o_ref = rest
    else:
        (o_ref,) = rest
    x = x_ref[...]
    y = x * lax.rsqrt(jnp.mean(x * x, axis=-1, keepdims=True) + EPS) * g_ref[...]
    if modulate:
        y = y * (1.0 + sc_ref[0]) + sh_ref[0]
    o_ref[...] = y.astype(o_ref.dtype)


def _norm(x2, g, seq, scale=None, shift=None, out_dtype=BF16):
    n, d = x2.shape
    tm = _tile(seq, NORM_ROWS, SUBLANES)
    modulate = scale is not None
    in_specs = [pl.BlockSpec((tm, d), lambda i: (i, 0)),
                pl.BlockSpec((1, d), lambda i: (0, 0))]
    args = [x2, g.reshape(1, d)]
    if modulate:
        per_batch = pl.BlockSpec((1, 1, d), lambda i: (i * tm // seq, 0, 0))
        in_specs += [per_batch, per_batch]
        args += [scale, shift]
    return pl.pallas_call(
        functools.partial(_norm_kernel, modulate=modulate),
        grid=(n // tm,),
        in_specs=in_specs,
        out_specs=pl.BlockSpec((tm, d), lambda i: (i, 0)),
        out_shape=jax.ShapeDtypeStruct((n, d), out_dtype),
        compiler_params=_params("parallel"),
        name="norm_mod" if modulate else "rms_norm",
    )(*args)


def _mm_gated_kernel(a_ref, w1_ref, w2_ref, *rest, act, has_bias, has_scale):
    rest = list(rest)
    o_ref = rest.pop()
    a = a_ref[...]
    z1 = _dot(a, w1_ref[0])
    z2 = _dot(a, w2_ref[0])
    if has_bias:
        z1 = z1 + rest[0][...]
        z2 = z2 + rest[1][...]
    if act == "glu":
        y = z1 * _sigmoid(z2)
    else:
        y = (z1 * _sigmoid(z1)) * z2
    if has_scale:
        y = y * rest[-1][0]
    o_ref[...] = y.astype(o_ref.dtype)


def _mm_gated(a, w1, w1_map, w2, w2_map, n_out, *, act, out_dtype, group_cols=None,
              biases=None, scale=None, scale_map=None):
    m, k = a.shape
    bm = _tile(m, ROW_TILE, SUBLANES)
    bn = _tile(n_out if group_cols is None else group_cols, COL_TILE_FUSED, LANES)
    in_specs = [
        pl.BlockSpec((bm, k), lambda i, j: (i, 0)),
        pl.BlockSpec((1, k, bn), lambda i, j: w1_map(j, bn)),
        pl.BlockSpec((1, k, bn), lambda i, j: w2_map(j, bn)),
    ]
    args = [a, w1, w2]
    if biases is not None:
        (b1, b1_map), (b2, b2_map) = biases
        in_specs += [pl.BlockSpec((1, bn), lambda i, j: b1_map(j, bn)),
                     pl.BlockSpec((1, bn), lambda i, j: b2_map(j, bn))]
        args += [b1, b2]
    if scale is not None:
        in_specs.append(pl.BlockSpec((1, bm, 1), lambda i, j: scale_map(i, j, bn)))
        args.append(scale)
    return pl.pallas_call(
        functools.partial(_mm_gated_kernel, act=act, has_bias=biases is not None,
                          has_scale=scale is not None),
        grid=(m // bm, n_out // bn),
        in_specs=in_specs,
        out_specs=pl.BlockSpec((bm, bn), lambda i, j: (i, j)),
        out_shape=jax.ShapeDtypeStruct((m, n_out), out_dtype),
        compiler_params=_params("parallel", "parallel"),
        name="mm_" + act,
    )(*args)


def _mm_kernel(a_ref, w_ref, *rest, nk, has_bias, has_res):
    rest = list(rest)
    acc_ref = rest.pop() if nk > 1 else None
    o_ref = rest.pop()
    part = _dot(a_ref[...], w_ref[...])

    def finish(acc):
        idx = 0
        if has_bias:
            acc = acc + rest[idx][...]
            idx += 1
        if has_res:
            acc = rest[idx][...] + rest[idx + 1][0] * acc
        o_ref[...] = acc.astype(o_ref.dtype)

    if nk == 1:
        finish(part)
    else:
        kk = pl.program_id(2)

        @pl.when(kk == 0)
        def _():
            acc_ref[...] = part

        @pl.when(kk > 0)
        def _():
            acc_ref[...] += part

        @pl.when(kk == nk - 1)
        def _():
            finish(acc_ref[...])


def _mm(a, w, *, out_dtype, bias=None, res=None, gate=None, seq=None):
    m, k = a.shape
    n_out = w.shape[1]
    bm = _tile(m if seq is None else seq, ROW_TILE, SUBLANES)
    bn = _tile(n_out, COL_TILE_SINGLE if res is None else COL_TILE_FUSED, LANES)
    bk = k if k <= K_SINGLE_MAX else _tile(k, K_TILE_SPLIT, LANES)
    nk = k // bk
    in_specs = [pl.BlockSpec((bm, bk), lambda i, j, kk: (i, kk)),
                pl.BlockSpec((bk, bn), lambda i, j, kk: (kk, j))]
    args = [a, w]
    if bias is not None:
        in_specs.append(pl.BlockSpec((1, bn), lambda i, j, kk: (0, j)))
        args.append(bias.reshape(1, n_out))
    if res is not None:
        in_specs += [pl.BlockSpec((bm, bn), lambda i, j, kk: (i, j)),
                     pl.BlockSpec((1, 1, bn), lambda i, j, kk: (i * bm // seq, 0, j))]
        args += [res, gate]
    return pl.pallas_call(
        functools.partial(_mm_kernel, nk=nk, has_bias=bias is not None, has_res=res is not None),
        grid=(m // bm, n_out // bn, nk),
        in_specs=in_specs,
        out_specs=pl.BlockSpec((bm, bn), lambda i, j, kk: (i, j)),
        out_shape=jax.ShapeDtypeStruct((m, n_out), out_dtype),
        scratch_shapes=[pltpu.VMEM((bm, bn), F32)] if nk > 1 else [],
        compiler_params=_params("parallel", "parallel", "arbitrary"),
        name="mm_res" if res is not None else "mm",
    )(*args)


def _conv_kernel(u_ref, halo_ref, w_ref, bdw_ref, g_ref, b_ref, o_ref, buf_ref, y_ref, *, ts, kw, lc):
    d = y_ref.shape[1]
    s = pl.program_id(1)

    @pl.when(s == 0)
    def _():
        buf_ref[0:CONV_HALO, :] = jnp.zeros((CONV_HALO, d), F32)

    @pl.when(s > 0)
    def _():
        buf_ref[0:CONV_HALO, :] = halo_ref[0]

    buf_ref[CONV_HALO:, :] = u_ref[0]
    base = CONV_HALO - (kw - 1)

    def lane_chunk(c, carry):
        l0 = pl.multiple_of(c * lc, lc)
        lanes = pl.ds(l0, lc)
        bias = bdw_ref[:, lanes]
        win_rows = CONV_GROUP + CONV_HALO
        for g0 in range(0, ts, CONV_GROUP):
            win = buf_ref[pl.ds(g0, win_rows), lanes]
            acc = jnp.broadcast_to(bias, (CONV_GROUP, lc))
            for r in range(SUBLANES):
                shifted = win if r == 0 else pltpu.roll(win, win_rows - r, axis=0)
                for a in range(CONV_HALO // SUBLANES + 1):
                    j = SUBLANES * a + r
                    if base <= j < base + kw:
                        acc = acc + w_ref[j - base:j - base + 1, lanes] * shifted[SUBLANES * a:SUBLANES * a + CONV_GROUP]
            y_ref[pl.ds(g0, CONV_GROUP), lanes] = acc
        return carry

    lax.fori_loop(0, d // lc, lane_chunk, 0)

    y = y_ref[...]
    mu = jnp.mean(y, axis=-1, keepdims=True)
    yc = y - mu
    var = jnp.mean(yc * yc, axis=-1, keepdims=True)
    z = yc * lax.rsqrt(var + EPS) * g_ref[...] + b_ref[...]
    o_ref[0] = (z * _sigmoid(z)).astype(o_ref.dtype)


def _conv_ln_silu(u3, w_dw, b_dw, ln_g, ln_b):
    b, s, d = u3.shape
    kw = w_dw.shape[0]
    assert kw - 1 <= CONV_HALO
    ts = _tile(s, CONV_ROWS, CONV_GROUP)
    halo_blocks = ts // CONV_HALO
    vec = pl.BlockSpec((1, d), lambda bi, si: (0, 0))
    return pl.pallas_call(
        functools.partial(_conv_kernel, ts=ts, kw=kw, lc=LANES),
        grid=(b, s // ts),
        in_specs=[
            pl.BlockSpec((1, ts, d), lambda bi, si: (bi, si, 0)),
            pl.BlockSpec((1, CONV_HALO, d), lambda bi, si: (bi, jnp.maximum(si * halo_blocks - 1, 0), 0)),
            pl.BlockSpec((kw, d), lambda bi, si: (0, 0)),
            vec, vec, vec,
        ],
        out_specs=pl.BlockSpec((1, ts, d), lambda bi, si: (bi, si, 0)),
        out_shape=jax.ShapeDtypeStruct((b, s, d), BF16),
        scratch_shapes=[pltpu.VMEM((ts + CONV_HALO, d), F32), pltpu.VMEM((ts, d), F32)],
        compiler_params=_params("parallel", "arbitrary"),
        name="dwconv_ln_silu",
    )(u3, u3, w_dw, b_dw.reshape(1, d), ln_g.reshape(1, d), ln_b.reshape(1, d))


def _gates_kernel(a_ref, w_ref, b_ref, o_ref, *, heads):
    z = _dot(a_ref[...], w_ref[...]) + b_ref[...]
    lane = lax.broadcasted_iota(jnp.int32, z.shape, 1)
    log_sig = -(jnp.maximum(-z, 0.0) + jnp.log1p(jnp.exp(-jnp.abs(z))))
    o_ref[...] = jnp.where((lane >= heads) & (lane < 2 * heads), log_sig, z)


def _mlstm_gates(h, w_pad, b_pad, heads):
    m, k = h.shape
    bm = _tile(m, ROW_TILE, SUBLANES)
    return pl.pallas_call(
        functools.partial(_gates_kernel, heads=heads),
        grid=(m // bm,),
        in_specs=[pl.BlockSpec((bm, k), lambda i: (i, 0)),
                  pl.BlockSpec((k, LANES), lambda i: (0, 0)),
                  pl.BlockSpec((1, LANES), lambda i: (0, 0))],
        out_specs=pl.BlockSpec((bm, LANES), lambda i: (i, 0)),
        out_shape=jax.ShapeDtypeStruct((m, LANES), F32),
        compiler_params=_params("parallel"),
        name="mlstm_gates",
    )(h, w_pad, b_pad)


def _mlstm_kernel(q_ref, k_ref, v_ref, o_ref, igr_ref, lfr_ref, igc_ref, lfc_ref, ng_ref,
                  out_ref, c_ref, n_ref, m_ref, *, chunk, dk):
    @pl.when(pl.program_id(2) == 0)
    def _():
        c_ref[...] = jnp.zeros_like(c_ref)
        n_ref[...] = jnp.zeros_like(n_ref)
        m_ref[...] = jnp.zeros_like(m_ref)

    q = q_ref[...]
    k = k_ref[...]
    v = v_ref[...]
    ig_r, lf_r = igr_ref[...], lfr_ref[...]
    ig_c, lf_c = igc_ref[...], lfc_ref[...]
    row = lax.broadcasted_iota(jnp.int32, (chunk, chunk), 0)
    col = lax.broadcasted_iota(jnp.int32, (chunk, chunk), 1)
    causal = col <= row
    b_c = jnp.sum(jnp.where(causal, lf_r, 0.0), axis=1, keepdims=True)
    b_r = jnp.sum(jnp.where(row <= col, lf_c, 0.0), axis=0, keepdims=True)
    b_end = jnp.sum(lf_r, axis=1, keepdims=True)
    m_prev = m_ref[...]
    a_c = b_c + m_prev
    dmat = jnp.where(causal, b_c - b_r + ig_r, -jnp.inf)
    m_s = jnp.maximum(a_c, jnp.max(dmat, axis=1, keepdims=True))
    w_inter = jnp.exp(a_c - m_s)
    scale = dk ** -0.5
    p = jnp.exp(dmat - m_s) * scale
    s = lax.dot_general(q, k, (((1,), (1,)), ((), ())), preferred_element_type=F32) * p
    c_state = c_ref[...]
    n_state = n_ref[...]
    num = _dot(s.astype(BF16), v) + w_inter * _dot(q, c_state.astype(BF16))
    den = (jnp.sum(s, axis=1, keepdims=True)
           + w_inter * jnp.sum(q.astype(F32) * n_state, axis=1, keepdims=True))
    h = num / jnp.maximum(jnp.abs(den), jnp.exp(-m_s))
    hn = h * lax.rsqrt(jnp.mean(h * h, axis=1, keepdims=True) + EPS) * ng_ref[...]
    out_ref[...] = (hn * _sigmoid(o_ref[...].astype(F32))).astype(out_ref.dtype)

    g_c = b_end - b_c + ig_c
    m_new = jnp.maximum(b_end + m_prev, jnp.max(g_c, axis=0, keepdims=True))
    decay = jnp.exp(b_end + m_prev - m_new)
    wk = k.astype(F32) * (jnp.exp(g_c - m_new) * scale)
    c_ref[...] = decay * c_state + lax.dot_general(
        wk.astype(BF16), v, (((0,), (0,)), ((), ())), preferred_element_type=F32)
    n_ref[...] = decay * n_state + jnp.sum(wk, axis=0, keepdims=True)
    m_ref[...] = m_new


def _mlstm_cell(proj, gates_rows, gates_cols, norm_g, batch, seq, heads, dk, dv):
    n = proj.shape[0]
    chunk = _tile(seq, MLSTM_CHUNK, LANES)
    nc = seq // chunk
    v_off = 2 * heads * dk // dv
    row_blk = lambda b, h, c: b * nc + c
    g_row = lambda off: pl.BlockSpec((None, None, None, 1, chunk), lambda b, h, c: (b, off + h, c, 0, 0))
    g_col = lambda off: pl.BlockSpec((None, None, None, chunk, 1), lambda b, h, c: (b, off + h, c, 0, 0))
    return pl.pallas_call(
        functools.partial(_mlstm_kernel, chunk=chunk, dk=dk),
        grid=(batch, heads, nc),
        in_specs=[
            pl.BlockSpec((chunk, dk), lambda b, h, c: (row_blk(b, h, c), h)),
            pl.BlockSpec((chunk, dk), lambda b, h, c: (row_blk(b, h, c), heads + h)),
            pl.BlockSpec((chunk, dv), lambda b, h, c: (row_blk(b, h, c), v_off + h)),
            pl.BlockSpec((chunk, dv), lambda b, h, c: (row_blk(b, h, c), v_off + heads + h)),
            g_row(0), g_row(heads), g_col(0), g_col(heads),
            pl.BlockSpec((1, dv), lambda b, h, c: (0, h)),
        ],
        out_specs=pl.BlockSpec((chunk, dv), lambda b, h, c: (row_blk(b, h, c), h)),
        out_shape=jax.ShapeDtypeStruct((n, heads * dv), BF16),
        scratch_shapes=[pltpu.VMEM((dk, dv), F32), pltpu.VMEM((1, dk), F32), pltpu.VMEM((1, 1), F32)],
        compiler_params=_params("parallel", "parallel", "arbitrary"),
        name="mlstm_cell",
    )(proj, proj, proj, proj, gates_rows, gates_rows, gates_cols, gates_cols, norm_g.reshape(1, heads * dv))


def _pad_cols(w, width):
    return jnp.pad(w, ((0, 0), (0, width - w.shape[1])))


def _router_kernel(a_ref, w_ref, b_ref, meta_ref, cnt_ref, carry_ref, *, experts):
    @pl.when(pl.program_id(0) == 0)
    def _():
        carry_ref[...] = jnp.zeros_like(carry_ref)

    logits = jnp.dot(a_ref[...], w_ref[...], preferred_element_type=F32,
                     precision=lax.Precision.HIGHEST) + b_ref[...]
    rows = logits.shape[0]
    lane = lax.broadcasted_iota(jnp.int32, logits.shape, 1)
    neg = -jnp.inf
    logits = jnp.where(lane < experts, logits, neg)
    v1 = jnp.max(logits, axis=1, keepdims=True)
    i1 = jnp.min(jnp.where(logits == v1, lane, LANES), axis=1, keepdims=True)
    rest = jnp.where(lane == i1, neg, logits)
    v2 = jnp.max(rest, axis=1, keepdims=True)
    i2 = jnp.min(jnp.where(rest == v2, lane, LANES), axis=1, keepdims=True)
    e2 = jnp.exp(v2 - v1)
    w1 = 1.0 / (1.0 + e2)
    w2 = e2 / (1.0 + e2)

    oh1 = jnp.where(lane == i1, 1.0, 0.0)
    oh2 = jnp.where(lane == i2, 1.0, 0.0)
    oh = oh1 + oh2
    tr = lax.broadcasted_iota(jnp.int32, (rows, rows), 0)
    tc = lax.broadcasted_iota(jnp.int32, (rows, rows), 1)
    tri = jnp.where(tc < tr, 1.0, 0.0).astype(BF16)
    rank = _dot(tri, oh.astype(BF16)) + carry_ref[...]
    r1 = jnp.sum(oh1 * rank, axis=1, keepdims=True)
    r2 = jnp.sum(oh2 * rank, axis=1, keepdims=True)
    carry_ref[...] += jnp.sum(oh, axis=0, keepdims=True)
    cnt_ref[...] = carry_ref[...]

    fields = (i1.astype(F32), i2.astype(F32), w1, w2, r1, r2)
    meta = jnp.zeros(logits.shape, F32)
    for idx, val in enumerate(fields):
        meta = jnp.where(lane == idx, val, meta)
    meta_ref[...] = meta


def _router(h, w_pad, b_pad, experts):
    m, k = h.shape
    bm = _tile(m, ROW_TILE, SUBLANES)
    return pl.pallas_call(
        functools.partial(_router_kernel, experts=experts),
        grid=(m // bm,),
        in_specs=[pl.BlockSpec((bm, k), lambda i: (i, 0)),
                  pl.BlockSpec((k, LANES), lambda i: (0, 0)),
                  pl.BlockSpec((1, LANES), lambda i: (0, 0))],
        out_specs=[pl.BlockSpec((bm, LANES), lambda i: (i, 0)),
                   pl.BlockSpec((1, LANES), lambda i: (0, 0))],
        out_shape=[jax.ShapeDtypeStruct((m, LANES), F32), jax.ShapeDtypeStruct((1, LANES), F32)],
        scratch_shapes=[pltpu.VMEM((1, LANES), F32)],
        compiler_params=_params("arbitrary"),
        name="moe_router",
    )(h, w_pad, b_pad)


def _row_copy(src_ref, src_row, dst_ref, dst_row, sem):
    return pltpu.make_async_copy(src_ref.at[pl.ds(src_row, 1)], dst_ref.at[pl.ds(dst_row, 1)], sem)


def _dispatch_kernel(valid_end_ref, pad_end_ref, pos1_ref, pos2_ref, h_ref, xs_ref, zero_ref, sem, zero_sem,
                     *, rows, experts):
    @pl.when(pl.program_id(0) == 0)
    def _():
        zero_ref[...] = jnp.zeros_like(zero_ref)
        for e in range(experts):
            lo, hi = valid_end_ref[e], pad_end_ref[e]

            def fill(p, carry):
                _row_copy(zero_ref, 0, xs_ref, p, zero_sem).start()
                return carry

            def fill_wait(p, carry):
                _row_copy(zero_ref, 0, xs_ref, p, zero_sem).wait()
                return carry

            lax.fori_loop(lo, hi, fill, 0)
            lax.fori_loop(lo, hi, fill_wait, 0)

        zr = zero_ref.shape[0]
        tail = pltpu.make_async_copy
        first = pad_end_ref[experts - 1] // zr
        last = xs_ref.shape[0] // zr

        def tail_fill(b, carry):
            tail(zero_ref, xs_ref.at[pl.ds(pl.multiple_of(b * zr, zr), zr)], zero_sem).start()
            return carry

        def tail_wait(b, carry):
            tail(zero_ref, xs_ref.at[pl.ds(pl.multiple_of(b * zr, zr), zr)], zero_sem).wait()
            return carry

        lax.fori_loop(first, last, tail_fill, 0)
        lax.fori_loop(first, last, tail_wait, 0)

    def issue(r, carry):
        _row_copy(h_ref, r, xs_ref, pos1_ref[0, 0, r], sem).start()
        _row_copy(h_ref, r, xs_ref, pos2_ref[0, 0, r], sem).start()
        return carry

    def drain(r, carry):
        _row_copy(h_ref, r, xs_ref, pos1_ref[0, 0, r], sem).wait()
        _row_copy(h_ref, r, xs_ref, pos2_ref[0, 0, r], sem).wait()
        return carry

    lax.fori_loop(0, rows, issue, 0)
    lax.fori_loop(0, rows, drain, 0)


def _dispatch(h, pos1, pos2, valid_end, pad_end, total_rows, bm):
    n, d = h.shape
    rows = _tile(n, GATHER_ROWS, SUBLANES)
    steps = n // rows
    experts = valid_end.shape[0]
    zero_rows = _tile(bm, ZERO_ROWS, SUBLANES)
    pos_spec = pl.BlockSpec((1, 1, rows), lambda i, ve, pe: (i, 0, 0), memory_space=pltpu.SMEM)
    return pl.pallas_call(
        functools.partial(_dispatch_kernel, rows=rows, experts=experts),
        grid_spec=pltpu.PrefetchScalarGridSpec(
            num_scalar_prefetch=2,
            grid=(steps,),
            in_specs=[pos_spec, pos_spec, pl.BlockSpec((rows, d), lambda i, ve, pe: (i, 0))],
            out_specs=pl.BlockSpec(memory_space=pl.ANY),
            scratch_shapes=[pltpu.VMEM((zero_rows, d), F32), pltpu.SemaphoreType.DMA, pltpu.SemaphoreType.DMA],
        ),
        out_shape=jax.ShapeDtypeStruct((total_rows, d), F32),
        compiler_params=_params("arbitrary"),
        name="moe_dispatch",
    )(valid_end, pad_end, pos1.reshape(steps, 1, rows), pos2.reshape(steps, 1, rows), h)


def _moe_up_kernel(be_ref, nu_ref, a_ref, wg_ref, wu_ref, o_ref):
    used = pl.program_id(0) < nu_ref[0]

    @pl.when(used)
    def _():
        a = a_ref[...].astype(BF16)
        z1 = _dot(a, wg_ref[0])
        z2 = _dot(a, wu_ref[0])
        o_ref[...] = ((z1 * _sigmoid(z1)) * z2).astype(o_ref.dtype)

    @pl.when(jnp.logical_not(used))
    def _():
        o_ref[...] = jnp.zeros_like(o_ref)


def _moe_down_kernel(be_ref, nu_ref, g_ref, w_ref, o_ref):
    used = pl.program_id(0) < nu_ref[0]

    @pl.when(used)
    def _():
        o_ref[...] = _dot(g_ref[...], w_ref[0])

    @pl.when(jnp.logical_not(used))
    def _():
        o_ref[...] = jnp.zeros_like(o_ref)


def _moe_experts(xs, block_expert, n_used, w_gate, w_up, w_down, bm):
    p, d = xs.shape
    f = w_gate.shape[2]
    nblk = p // bm

    def row_idx(i, nu):
        return jnp.minimum(i, nu[0] - 1)

    bn = _tile(f, COL_TILE_FUSED, LANES)
    nj = f // bn
    w_spec = pl.BlockSpec((1, d, bn), lambda i, j, be, nu: (be[i], 0, jnp.where(i < nu[0], j, nj - 1)))
    g = pl.pallas_call(
        _moe_up_kernel,
        grid_spec=pltpu.PrefetchScalarGridSpec(
            num_scalar_prefetch=2,
            grid=(nblk, nj),
            in_specs=[pl.BlockSpec((bm, d), lambda i, j, be, nu: (row_idx(i, nu), 0)), w_spec, w_spec],
            out_specs=pl.BlockSpec((bm, bn), lambda i, j, be, nu: (i, j)),
        ),
        out_shape=jax.ShapeDtypeStruct((p, f), BF16),
        compiler_params=_params("parallel", "arbitrary"),
        name="moe_up",
    )(block_expert, n_used, xs, w_gate, w_up)

    bn2 = _tile(d, COL_TILE_SINGLE, LANES)
    nj2 = d // bn2
    return pl.pallas_call(
        _moe_down_kernel,
        grid_spec=pltpu.PrefetchScalarGridSpec(
            num_scalar_prefetch=2,
            grid=(nblk, nj2),
            in_specs=[pl.BlockSpec((bm, f), lambda i, j, be, nu: (row_idx(i, nu), 0)),
                      pl.BlockSpec((1, f, bn2), lambda i, j, be, nu: (be[i], 0, jnp.where(i < nu[0], j, nj2 - 1)))],
            out_specs=pl.BlockSpec((bm, bn2), lambda i, j, be, nu: (i, j)),
        ),
        out_shape=jax.ShapeDtypeStruct((p, d), F32),
        compiler_params=_params("parallel", "arbitrary"),
        name="moe_down",
    )(block_expert, n_used, g, w_down)


def _combine_kernel(pos1_ref, pos2_ref, x_ref, meta_ref, gate_ref, y_ref, o_ref, buf1_ref, buf2_ref, sem, *, rows):
    def issue(r, carry):
        _row_copy(y_ref, pos1_ref[0, 0, r], buf1_ref, r, sem).start()
        _row_copy(y_ref, pos2_ref[0, 0, r], buf2_ref, r, sem).start()
        return carry

    def drain(r, carry):
        _row_copy(y_ref, pos1_ref[0, 0, r], buf1_ref, r, sem).wait()
        _row_copy(y_ref, pos2_ref[0, 0, r], buf2_ref, r, sem).wait()
        return carry

    lax.fori_loop(0, rows, issue, 0)
    lax.fori_loop(0, rows, drain, 0)
    w1 = meta_ref[:, 2:3]
    w2 = meta_ref[:, 3:4]
    o_ref[...] = x_ref[...] + gate_ref[0] * (w1 * buf1_ref[...] + w2 * buf2_ref[...])


def _combine(x2, y, pos1, pos2, meta, gate, seq):
    n, d = x2.shape
    rows = _tile(seq, GATHER_ROWS, SUBLANES)
    steps = n // rows
    pos_spec = pl.BlockSpec((1, 1, rows), lambda i: (i, 0, 0), memory_space=pltpu.SMEM)
    return pl.pallas_call(
        functools.partial(_combine_kernel, rows=rows),
        grid=(steps,),
        in_specs=[pos_spec, pos_spec,
                  pl.BlockSpec((rows, d), lambda i: (i, 0)),
                  pl.BlockSpec((rows, LANES), lambda i: (i, 0)),
                  pl.BlockSpec((1, 1, d), lambda i: (i * rows // seq, 0, 0)),
                  pl.BlockSpec(memory_space=pl.ANY)],
        out_specs=pl.BlockSpec((rows, d), lambda i: (i, 0)),
        out_shape=jax.ShapeDtypeStruct((n, d), F32),
        scratch_shapes=[pltpu.VMEM((rows, d), F32), pltpu.VMEM((rows, d), F32), pltpu.SemaphoreType.DMA],
        compiler_params=_params("arbitrary"),
        name="moe_combine",
    )(pos1.reshape(steps, 1, rows), pos2.reshape(steps, 1, rows), x2, meta, gate, y)


def _moe_layer(x2, h, gate, seq, w_router, b_router, w_gate, w_up, w_down):
    n, d = h.shape
    experts = w_router.shape[1]
    meta, counts = _router(h, _pad_cols(w_router, LANES), _pad_cols(b_router.reshape(1, experts), LANES), experts)
    bm = _tile(2 * n, MOE_ROWS, SUBLANES)
    total_rows = 2 * n + experts * bm
    i1, i2 = meta[:, 0].astype(jnp.int32), meta[:, 1].astype(jnp.int32)
    r1, r2 = meta[:, 4].astype(jnp.int32), meta[:, 5].astype(jnp.int32)
    cnt = counts[0, :experts].astype(jnp.int32)
    padded = (cnt + bm - 1) // bm * bm
    pad_end = jnp.cumsum(padded)
    start = pad_end - padded
    valid_end = start + cnt
    pos1 = start[i1] + r1
    pos2 = start[i2] + r2
    blk_start = jnp.arange(total_rows // bm, dtype=jnp.int32) * bm
    block_expert = jnp.minimum(jnp.sum(blk_start[:, None] >= pad_end[None, :], axis=1), experts - 1).astype(jnp.int32)
    n_used = (pad_end[-1:] // bm).astype(jnp.int32)

    xs = _dispatch(h, pos1, pos2, valid_end, pad_end, total_rows, bm)
    y = _moe_experts(xs, block_expert, n_used, w_gate.astype(BF16), w_up.astype(BF16), w_down.astype(BF16), bm)
    return _combine(x2, y, pos1, pos2, meta, gate, seq)


def kernel(x, c, cond_w, cond_b, ada_w, ada_b, mix_norm_g, ffn_norm_g, final_norm_g, conv_w_in, conv_b_in, conv_w_dw, conv_b_dw, conv_ln_g, conv_ln_b, conv_w_out, conv_b_out, mlstm_w_in, mlstm_b_gates, mlstm_norm_g, mlstm_w_out, ffn_w_gate, ffn_w_up, ffn_w_down, moe_w_router, moe_b_router, moe_w_gate, moe_w_up, moe_w_down):
    batch, seq, d = x.shape
    n = batch * seq
    depth = ada_w.shape[0]
    heads = mlstm_b_gates.shape[1] // 2
    dv = mlstm_norm_g.shape[1] // heads
    dk = (mlstm_w_in.shape[2] - 2 * heads * dv - 2 * heads) // (2 * heads)
    experts = moe_w_router.shape[2]
    d_exp = moe_w_gate.shape[3]
    assert dk % LANES == 0 and dv % LANES == 0 and 2 * heads <= LANES and experts <= LANES

    c_pad = jnp.pad(c, ((0, SUBLANES - batch % SUBLANES if batch % SUBLANES else 0), (0, 0)))
    e = _cond_embed(c_pad, cond_w, cond_b)
    mods = _cond_mod(e, ada_w, ada_b)[:, :batch].reshape(depth, batch, 6, 1, d)

    xf = x.reshape(n, d)
    for i in range(depth):
        j = i // 2
        shift1, scale1, gate1, shift2, scale2, gate2 = [mods[i, :, t] for t in range(6)]

        h = _norm(xf, mix_norm_g[i], seq, scale1, shift1)
        if i % 2 == 0:
            w_in = conv_w_in[j].astype(BF16).reshape(1, d, 2 * d)
            b_in = conv_b_in[j].reshape(1, 2 * d)
            u = _mm_gated(
                h, w_in, lambda jb, bn: (0, 0, jb), w_in, lambda jb, bn: (0, 0, jb + d // bn), d,
                act="glu", out_dtype=F32,
                biases=((b_in, lambda jb, bn: (0, jb)), (b_in, lambda jb, bn: (0, jb + d // bn))))
            v = _conv_ln_silu(u.reshape(batch, seq, d), conv_w_dw[j], conv_b_dw[j], conv_ln_g[j], conv_ln_b[j])
            xf = _mm(v.reshape(n, d), conv_w_out[j].astype(BF16), out_dtype=F32,
                     bias=conv_b_out[j], res=xf, gate=gate1, seq=seq)
        else:
            n_main = 2 * heads * (dk + dv)
            w_main = mlstm_w_in[j][:, :n_main].astype(BF16)
            w_gates = _pad_cols(mlstm_w_in[j][:, n_main:], LANES).astype(BF16)
            b_gates = _pad_cols(mlstm_b_gates[j].reshape(1, 2 * heads), LANES)
            proj = _mm(h, w_main, out_dtype=BF16)
            gates = _mlstm_gates(h, w_gates, b_gates, heads)[:, :2 * heads]
            chunk = _tile(seq, MLSTM_CHUNK, LANES)
            gates_t = gates.reshape(batch, seq // chunk, chunk, 2 * heads).transpose(0, 3, 1, 2)
            y = _mlstm_cell(proj, gates_t[:, :, :, None, :], gates_t[:, :, :, :, None],
                            mlstm_norm_g[j], batch, seq, heads, dk, dv)
            xf = _mm(y, mlstm_w_out[j].astype(BF16), out_dtype=F32, res=xf, gate=gate1, seq=seq)

        h = _norm(xf, ffn_norm_g[i], seq, scale2, shift2, out_dtype=BF16 if i % 2 == 0 else F32)
        if i % 2 == 0:
            d_ff = ffn_w_gate.shape[2]
            wg = ffn_w_gate[j].astype(BF16).reshape(1, d, d_ff)
            wu = ffn_w_up[j].astype(BF16).reshape(1, d, d_ff)
            col = lambda jb, bn: (0, 0, jb)
            g = _mm_gated(h, wg, col, wu, col, d_ff, act="swiglu", out_dtype=BF16)
            xf = _mm(g, ffn_w_down[j].astype(BF16), out_dtype=F32, res=xf, gate=gate2, seq=seq)
        else:
            xf = _moe_layer(xf, h, gate2, seq, moe_w_router[j], moe_b_router[j],
                            moe_w_gate[j], moe_w_up[j], moe_w_down[j])

    out = _norm(xf, final_norm_g, seq, out_dtype=F32)
    return out.reshape(batch, seq, d)
```

```python
import functools

import jax
import jax.numpy as jnp
from jax import lax
from jax.experimental import pallas as pl
from jax.experimental.pallas import tpu as pltpu

EPS = 1e-6
F32 = jnp.float32
BF16 = jnp.bfloat16

LANES = 128
SUBLANES = 8
VMEM_LIMIT_BYTES = 56 * 1024 * 1024

ROW_TILE = 1024
COL_TILE_FUSED = 512
COL_TILE_WIDE = 1024
LONG_K_ROWS = 512
MOE_DOWN_COLS = 2048
LN_ROWS = 16
LN_UNROLL = 4
NORM_ROWS = 256
CONV_ROWS = 256
CONV_HALO = 32
CONV_GROUP = 64
MLSTM_CHUNK = 256
MOE_ROWS = 512
GATHER_ROWS = 256
ZERO_ROWS = 64


def _tile(dim, preferred, align):
    if dim <= preferred:
        return dim
    t = preferred - preferred % align
    while t > align and dim % t:
        t -= align
    assert dim % t == 0, (dim, preferred, align)
    return t


def _params(*sem):
    return pltpu.CompilerParams(dimension_semantics=sem, vmem_limit_bytes=VMEM_LIMIT_BYTES)


def _sigmoid(z):
    return 1.0 / (1.0 + jnp.exp(-z))


def _dot(a, b):
    return jnp.dot(a, b, preferred_element_type=F32)


def _cond_embed_kernel(c_ref, w_ref, b_ref, o_ref):
    z = jnp.dot(c_ref[...], w_ref[...], preferred_element_type=F32,
                precision=lax.Precision.HIGHEST) + b_ref[...]
    o_ref[...] = z * _sigmoid(z)


def _cond_embed(c_pad, cond_w, cond_b):
    rows, r = c_pad.shape[0], cond_w.shape[1]
    return pl.pallas_call(
        _cond_embed_kernel,
        out_shape=jax.ShapeDtypeStruct((rows, r), F32),
        compiler_params=pltpu.CompilerParams(vmem_limit_bytes=VMEM_LIMIT_BYTES),
        name="cond_embed",
    )(c_pad, cond_w, cond_b.reshape(1, r))


def _cond_mod_kernel(e_ref, w_ref, b_ref, o_ref):
    o_ref[0] = jnp.dot(e_ref[...], w_ref[0], preferred_element_type=F32,
                       precision=lax.Precision.HIGHEST) + b_ref[0]


def _cond_mod(e, ada_w, ada_b):
    depth, r, width = ada_w.shape
    rows = e.shape[0]
    bn = _tile(width, 2048, LANES)
    return pl.pallas_call(
        _cond_mod_kernel,
        grid=(depth, width // bn),
        in_specs=[
            pl.BlockSpec((rows, r), lambda i, j: (0, 0)),
            pl.BlockSpec((1, r, bn), lambda i, j: (i, 0, j)),
            pl.BlockSpec((1, 1, bn), lambda i, j: (i, 0, j)),
        ],
        out_specs=pl.BlockSpec((1, rows, bn), lambda i, j: (i, 0, j)),
        out_shape=jax.ShapeDtypeStruct((depth, rows, width), F32),
        compiler_params=_params("parallel", "parallel"),
        name="cond_mod",
    )(e, ada_w, ada_b.reshape(depth, 1, width))


def _norm_kernel(x_ref, g_ref, *rest, modulate):
    if modulate:
        sc_ref, sh_ref, o_ref = rest
    else:
        (o_ref,) = rest
    x = x_ref[...]
    y = x * lax.rsqrt(jnp.mean(x * x, axis=-1, keepdims=True) + EPS) * g_ref[...]
    if modulate:
        y = y * (1.0 + sc_ref[0]) + sh_ref[0]
    o_ref[...] = y.astype(o_ref.dtype)


def _norm(x2, g, seq, scale=None, shift=None, out_dtype=BF16):
    n, d = x2.shape
    tm = _tile(seq, NORM_ROWS, SUBLANES)
    modulate = scale is not None
    in_specs = [pl.BlockSpec((tm, d), lambda i: (i, 0)),
                pl.BlockSpec((1, d), lambda i: (0, 0))]
    args = [x2, g.reshape(1, d)]
    if modulate:
        per_batch = pl.BlockSpec((1, 1, d), lambda i: (i * tm // seq, 0, 0))
        in_specs += [per_batch, per_batch]
        args += [scale, shift]
    return pl.pallas_call(
        functools.partial(_norm_kernel, modulate=modulate),
        grid=(n // tm,),
        in_specs=in_specs,
        out_specs=pl.BlockSpec((tm, d), lambda i: (i, 0)),
        out_shape=jax.ShapeDtypeStruct((n, d), out_dtype),
        compiler_params=_params("parallel"),
        name="norm_mod" if modulate else "rms_norm",
    )(*args)


def _stage_weights(w_hbm, layer, col_off, stage_ref, wb_ref, sem, *, bn, nj):
    j = pl.program_id(0)
    first_row_block = pl.program_id(1) == 0

    def fetch(jj):
        cols = pl.ds(pl.multiple_of((jj + col_off) * bn, bn), bn)
        return pltpu.make_async_copy(w_hbm.at[layer, :, cols], stage_ref, sem)

    @pl.when(jnp.logical_and(first_row_block, j == 0))
    def _():
        fetch(0).start()

    @pl.when(first_row_block)
    def _():
        fetch(j).wait()
        wb_ref[...] = stage_ref[...].astype(BF16)

        @pl.when(j + 1 < nj)
        def _():
            fetch(j + 1).start()


def _mm_gated_kernel(a_ref, w1_hbm, w2_hbm, *rest, act, has_bias, layer, o1, o2, bn, nj):
    rest = list(rest)
    sem = rest.pop()
    w2b_ref = rest.pop()
    w1b_ref = rest.pop()
    stage2_ref = rest.pop()
    stage1_ref = rest.pop()
    o_ref = rest.pop()
    _stage_weights(w1_hbm, layer, o1, stage1_ref, w1b_ref, sem.at[0], bn=bn, nj=nj)
    _stage_weights(w2_hbm, layer, o2, stage2_ref, w2b_ref, sem.at[1], bn=bn, nj=nj)

    a = a_ref[...]
    z1 = _dot(a, w1b_ref[...])
    z2 = _dot(a, w2b_ref[...])
    if has_bias:
        z1 = z1 + rest[0][0]
        z2 = z2 + rest[1][0]
    if act == "glu":
        y = z1 * _sigmoid(z2)
    else:
        y = (z1 * _sigmoid(z1)) * z2
    o_ref[...] = y.astype(o_ref.dtype)


def _mm_gated(a, w1, w2, layer, n_out, *, act, out_dtype, off1=0, off2=0, bias=None):
    m, k = a.shape
    bm = _tile(m, ROW_TILE, SUBLANES)
    bn = _tile(n_out, COL_TILE_FUSED, LANES)
    assert off1 % bn == 0 and off2 % bn == 0
    o1, o2 = off1 // bn, off2 // bn
    nj = n_out // bn
    in_specs = [
        pl.BlockSpec((bm, k), lambda j, i: (i, 0)),
        pl.BlockSpec(memory_space=pl.ANY),
        pl.BlockSpec(memory_space=pl.ANY),
    ]
    args = [a, w1, w2]
    if bias is not None:
        in_specs += [pl.BlockSpec((1, 1, bn), lambda j, i: (layer, 0, j + o1)),
                     pl.BlockSpec((1, 1, bn), lambda j, i: (layer, 0, j + o2))]
        args += [bias, bias]
    return pl.pallas_call(
        functools.partial(_mm_gated_kernel, act=act, has_bias=bias is not None,
                          layer=layer, o1=o1, o2=o2, bn=bn, nj=nj),
        grid=(nj, m // bm),
        in_specs=in_specs,
        out_specs=pl.BlockSpec((bm, bn), lambda j, i: (i, j)),
        out_shape=jax.ShapeDtypeStruct((m, n_out), out_dtype),
        scratch_shapes=[pltpu.VMEM((k, bn), F32), pltpu.VMEM((k, bn), F32),
                        pltpu.VMEM((k, bn), BF16), pltpu.VMEM((k, bn), BF16), pltpu.SemaphoreType.DMA((2,))],
        compiler_params=_params("arbitrary", "arbitrary"),
        name="mm_" + act,
    )(*args)


def _mm_ws_kernel(a_ref, w_hbm, *rest, has_bias, has_res, layer, bn, nj):
    rest = list(rest)
    sem = rest.pop()
    wb_ref = rest.pop()
    stage_ref = rest.pop()
    o_ref = rest.pop()
    _stage_weights(w_hbm, layer, 0, stage_ref, wb_ref, sem, bn=bn, nj=nj)

    acc = _dot(a_ref[...], wb_ref[...])
    idx = 0
    if has_bias:
        acc = acc + rest[idx][0]
        idx += 1
    if has_res:
        acc = rest[idx][...] + rest[idx + 1][0] * acc
    o_ref[...] = acc.astype(o_ref.dtype)


def _mm_ws(a, w, layer, n_out, *, out_dtype, bias=None, res=None, gate=None, seq=None):
    m, k = a.shape
    bm = _tile(m if seq is None else seq, ROW_TILE, SUBLANES)
    bn = _tile(n_out, COL_TILE_FUSED if res is not None else COL_TILE_WIDE, LANES)
    nj = n_out // bn
    in_specs = [pl.BlockSpec((bm, k), lambda j, i: (i, 0)),
                pl.BlockSpec(memory_space=pl.ANY)]
    args = [a, w]
    if bias is not None:
        in_specs.append(pl.BlockSpec((1, 1, bn), lambda j, i: (layer, 0, j)))
        args.append(bias)
    if res is not None:
        in_specs += [pl.BlockSpec((bm, bn), lambda j, i: (i, j)),
                     pl.BlockSpec((1, 1, bn), lambda j, i: (i * bm // seq, 0, j))]
        args += [res, gate]
    return pl.pallas_call(
        functools.partial(_mm_ws_kernel, has_bias=bias is not None, has_res=res is not None,
                          layer=layer, bn=bn, nj=nj),
        grid=(nj, m // bm),
        in_specs=in_specs,
        out_specs=pl.BlockSpec((bm, bn), lambda j, i: (i, j)),
        out_shape=jax.ShapeDtypeStruct((m, n_out), out_dtype),
        scratch_shapes=[pltpu.VMEM((k, bn), F32), pltpu.VMEM((k, bn), BF16), pltpu.SemaphoreType.DMA],
        compiler_params=_params("arbitrary", "arbitrary"),
        name="mm_res" if res is not None else "mm",
    )(*args)


def _mm_long_kernel(a_ref, w_ref, res_ref, gate_ref, o_ref):
    o_ref[...] = res_ref[...] + gate_ref[0] * _dot(a_ref[...], w_ref[0])


def _mm_long(a, w, layer, *, res, gate, seq):
    m, k = a.shape
    n_out = w.shape[2]
    bm = _tile(seq, LONG_K_ROWS, SUBLANES)
    bn = _tile(n_out, COL_TILE_FUSED, LANES)
    return pl.pallas_call(
        _mm_long_kernel,
        grid=(m // bm, n_out // bn),
        in_specs=[pl.BlockSpec((bm, k), lambda i, j: (i, 0)),
                  pl.BlockSpec((1, k, bn), lambda i, j: (layer, 0, j)),
                  pl.BlockSpec((bm, bn), lambda i, j: (i, j)),
                  pl.BlockSpec((1, 1, bn), lambda i, j: (i * bm // seq, 0, j))],
        out_specs=pl.BlockSpec((bm, bn), lambda i, j: (i, j)),
        out_shape=jax.ShapeDtypeStruct((m, n_out), F32),
        compiler_params=_params("parallel", "arbitrary"),
        name="mm_long",
    )(a, w, res, gate)


def _conv_kernel(u_ref, halo_ref, w_ref, bdw_ref, g_ref, b_ref, o_ref, buf_ref, y_ref, *, ts, kw, lc):
    d = y_ref.shape[1]
    s = pl.program_id(1)

    @pl.when(s == 0)
    def _():
        buf_ref[0:CONV_HALO, :] = jnp.zeros((CONV_HALO, d), F32)

    @pl.when(s > 0)
    def _():
        buf_ref[0:CONV_HALO, :] = halo_ref[0]

    buf_ref[CONV_HALO:, :] = u_ref[0]
    base = CONV_HALO - (kw - 1)

    def lane_chunk(c, carry):
        l0 = pl.multiple_of(c * lc, lc)
        lanes = pl.ds(l0, lc)
        bias = bdw_ref[:, lanes]
        win_rows = CONV_GROUP + CONV_HALO
        for g0 in range(0, ts, CONV_GROUP):
            win = buf_ref[pl.ds(g0, win_rows), lanes]
            acc = jnp.broadcast_to(bias, (CONV_GROUP, lc))
            for r in range(SUBLANES):
                shifted = win if r == 0 else pltpu.roll(win, win_rows - r, axis=0)
                for a in range(CONV_HALO // SUBLANES + 1):
                    j = SUBLANES * a + r
                    if base <= j < base + kw:
                        acc = acc + w_ref[j - base:j - base + 1, lanes] * shifted[SUBLANES * a:SUBLANES * a + CONV_GROUP]
            y_ref[pl.ds(g0, CONV_GROUP), lanes] = acc
        return carry

    lax.fori_loop(0, d // lc, lane_chunk, 0)

    def layer_norm_rows(t, carry):
        rows = pl.ds(pl.multiple_of(t * LN_ROWS, LN_ROWS), LN_ROWS)
        y = y_ref[rows, :]
        mu = jnp.mean(y, axis=-1, keepdims=True)
        yc = y - mu
        var = jnp.mean(yc * yc, axis=-1, keepdims=True)
        z = yc * lax.rsqrt(var + EPS) * g_ref[...] + b_ref[...]
        o_ref[0, rows, :] = (z * _sigmoid(z)).astype(o_ref.dtype)
        return carry

    lax.fori_loop(0, ts // LN_ROWS, layer_norm_rows, 0, unroll=LN_UNROLL)


def _conv_ln_silu(u3, w_dw, b_dw, ln_g, ln_b):
    b, s, d = u3.shape
    kw = w_dw.shape[0]
    assert kw - 1 <= CONV_HALO
    ts = _tile(s, CONV_ROWS, CONV_GROUP)
    assert ts % LN_ROWS == 0
    halo_blocks = ts // CONV_HALO
    vec = pl.BlockSpec((1, d), lambda bi, si: (0, 0))
    return pl.pallas_call(
        functools.partial(_conv_kernel, ts=ts, kw=kw, lc=LANES),
        grid=(b, s // ts),
        in_specs=[
            pl.BlockSpec((1, ts, d), lambda bi, si: (bi, si, 0)),
            pl.BlockSpec((1, CONV_HALO, d), lambda bi, si: (bi, jnp.maximum(si * halo_blocks - 1, 0), 0)),
            pl.BlockSpec((kw, d), lambda bi, si: (0, 0)),
            vec, vec, vec,
        ],
        out_specs=pl.BlockSpec((1, ts, d), lambda bi, si: (bi, si, 0)),
        out_shape=jax.ShapeDtypeStruct((b, s, d), BF16),
        scratch_shapes=[pltpu.VMEM((ts + CONV_HALO, d), F32), pltpu.VMEM((ts, d), F32)],
        compiler_params=_params("parallel", "arbitrary"),
        name="dwconv_ln_silu",
    )(u3, u3, w_dw, b_dw.reshape(1, d), ln_g.reshape(1, d), ln_b.reshape(1, d))


def _gates_kernel(a_ref, w_ref, b_ref, o_ref, *, heads):
    z = _dot(a_ref[...], w_ref[...]) + b_ref[...]
    lane = lax.broadcasted_iota(jnp.int32, z.shape, 1)
    log_sig = -(jnp.maximum(-z, 0.0) + jnp.log1p(jnp.exp(-jnp.abs(z))))
    o_ref[...] = jnp.where((lane >= heads) & (lane < 2 * heads), log_sig, z)


def _mlstm_gates(h, w_pad, b_pad, heads):
    m, k = h.shape
    bm = _tile(m, ROW_TILE, SUBLANES)
    return pl.pallas_call(
        functools.partial(_gates_kernel, heads=heads),
        grid=(m // bm,),
        in_specs=[pl.BlockSpec((bm, k), lambda i: (i, 0)),
                  pl.BlockSpec((k, LANES), lambda i: (0, 0)),
                  pl.BlockSpec((1, LANES), lambda i: (0, 0))],
        out_specs=pl.BlockSpec((bm, LANES), lambda i: (i, 0)),
        out_shape=jax.ShapeDtypeStruct((m, LANES), F32),
        compiler_params=_params("parallel"),
        name="mlstm_gates",
    )(h, w_pad, b_pad)


def _mlstm_kernel(q_ref, k_ref, v_ref, o_ref, igr_ref, lfr_ref, igc_ref, lfc_ref, ng_ref,
                  out_ref, c_ref, n_ref, m_ref, *, chunk, dk):
    @pl.when(pl.program_id(2) == 0)
    def _():
        c_ref[...] = jnp.zeros_like(c_ref)
        n_ref[...] = jnp.zeros_like(n_ref)
        m_ref[...] = jnp.zeros_like(m_ref)

    q = q_ref[...]
    k = k_ref[...]
    v = v_ref[...]
    ig_r, lf_r = igr_ref[...], lfr_ref[...]
    ig_c, lf_c = igc_ref[...], lfc_ref[...]
    row = lax.broadcasted_iota(jnp.int32, (chunk, chunk), 0)
    col = lax.broadcasted_iota(jnp.int32, (chunk, chunk), 1)
    causal = col <= row
    b_c = jnp.sum(jnp.where(causal, lf_r, 0.0), axis=1, keepdims=True)
    b_r = jnp.sum(jnp.where(row <= col, lf_c, 0.0), axis=0, keepdims=True)
    b_end = jnp.sum(lf_r, axis=1, keepdims=True)
    m_prev = m_ref[...]
    a_c = b_c + m_prev
    dmat = jnp.where(causal, b_c - b_r + ig_r, -jnp.inf)
    m_s = jnp.maximum(a_c, jnp.max(dmat, axis=1, keepdims=True))
    w_inter = jnp.exp(a_c - m_s)
    scale = dk ** -0.5
    p = jnp.exp(dmat - m_s) * scale
    s = lax.dot_general(q, k, (((1,), (1,)), ((), ())), preferred_element_type=F32) * p
    c_state = c_ref[...]
    n_state = n_ref[...]
    num = _dot(s.astype(BF16), v) + w_inter * _dot(q, c_state.astype(BF16))
    den = (jnp.sum(s, axis=1, keepdims=True)
           + w_inter * jnp.sum(q.astype(F32) * n_state, axis=1, keepdims=True))
    h = num / jnp.maximum(jnp.abs(den), jnp.exp(-m_s))
    hn = h * lax.rsqrt(jnp.mean(h * h, axis=1, keepdims=True) + EPS) * ng_ref[...]
    out_ref[...] = (hn * _sigmoid(o_ref[...].astype(F32))).astype(out_ref.dtype)

    g_c = b_end - b_c + ig_c
    m_new = jnp.maximum(b_end + m_prev, jnp.max(g_c, axis=0, keepdims=True))
    decay = jnp.exp(b_end + m_prev - m_new)
    wk = k.astype(F32) * (jnp.exp(g_c - m_new) * scale)
    c_ref[...] = decay * c_state + lax.dot_general(
        wk.astype(BF16), v, (((0,), (0,)), ((), ())), preferred_element_type=F32)
    n_ref[...] = decay * n_state + jnp.sum(wk, axis=0, keepdims=True)
    m_ref[...] = m_new


def _mlstm_cell(proj, gates_rows, gates_cols, norm_g, batch, seq, heads, dk, dv):
    n = proj.shape[0]
    chunk = _tile(seq, MLSTM_CHUNK, LANES)
    nc = seq // chunk
    v_off = 2 * heads * dk // dv
    row_blk = lambda b, h, c: b * nc + c
    g_row = lambda off: pl.BlockSpec((None, None, None, 1, chunk), lambda b, h, c: (b, off + h, c, 0, 0))
    g_col = lambda off: pl.BlockSpec((None, None, None, chunk, 1), lambda b, h, c: (b, off + h, c, 0, 0))
    return pl.pallas_call(
        functools.partial(_mlstm_kernel, chunk=chunk, dk=dk),
        grid=(batch, heads, nc),
        in_specs=[
            pl.BlockSpec((chunk, dk), lambda b, h, c: (row_blk(b, h, c), h)),
            pl.BlockSpec((chunk, dk), lambda b, h, c: (row_blk(b, h, c), heads + h)),
            pl.BlockSpec((chunk, dv), lambda b, h, c: (row_blk(b, h, c), v_off + h)),
            pl.BlockSpec((chunk, dv), lambda b, h, c: (row_blk(b, h, c), v_off + heads + h)),
            g_row(0), g_row(heads), g_col(0), g_col(heads),
            pl.BlockSpec((1, dv), lambda b, h, c: (0, h)),
        ],
        out_specs=pl.BlockSpec((chunk, dv), lambda b, h, c: (row_blk(b, h, c), h)),
        out_shape=jax.ShapeDtypeStruct((n, heads * dv), BF16),
        scratch_shapes=[pltpu.VMEM((dk, dv), F32), pltpu.VMEM((1, dk), F32), pltpu.VMEM((1, 1), F32)],
        compiler_params=_params("parallel", "parallel", "arbitrary"),
        name="mlstm_cell",
    )(proj, proj, proj, proj, gates_rows, gates_rows, gates_cols, gates_cols, norm_g.reshape(1, heads * dv))


def _pad_cols(w, width):
    return jnp.pad(w, ((0, 0), (0, width - w.shape[1])))


def _router_kernel(a_ref, w_ref, b_ref, meta_ref, cnt_ref, carry_ref, *, experts):
    @pl.when(pl.program_id(0) == 0)
    def _():
        carry_ref[...] = jnp.zeros_like(carry_ref)

    logits = jnp.dot(a_ref[...], w_ref[...], preferred_element_type=F32,
                     precision=lax.Precision.HIGHEST) + b_ref[...]
    rows = logits.shape[0]
    lane = lax.broadcasted_iota(jnp.int32, logits.shape, 1)
    neg = -jnp.inf
    logits = jnp.where(lane < experts, logits, neg)
    v1 = jnp.max(logits, axis=1, keepdims=True)
    i1 = jnp.min(jnp.where(logits == v1, lane, LANES), axis=1, keepdims=True)
    rest = jnp.where(lane == i1, neg, logits)
    v2 = jnp.max(rest, axis=1, keepdims=True)
    i2 = jnp.min(jnp.where(rest == v2, lane, LANES), axis=1, keepdims=True)
    e2 = jnp.exp(v2 - v1)
    w1 = 1.0 / (1.0 + e2)
    w2 = e2 / (1.0 + e2)

    oh1 = jnp.where(lane == i1, 1.0, 0.0)
    oh2 = jnp.where(lane == i2, 1.0, 0.0)
    oh = oh1 + oh2
    tr = lax.broadcasted_iota(jnp.int32, (rows, rows), 0)
    tc = lax.broadcasted_iota(jnp.int32, (rows, rows), 1)
    tri = jnp.where(tc < tr, 1.0, 0.0).astype(BF16)
    rank = _dot(tri, oh.astype(BF16)) + carry_ref[...]
    r1 = jnp.sum(oh1 * rank, axis=1, keepdims=True)
    r2 = jnp.sum(oh2 * rank, axis=1, keepdims=True)
    carry_ref[...] += jnp.sum(oh, axis=0, keepdims=True)
    cnt_ref[...] = carry_ref[...]

    fields = (i1.astype(F32), i2.astype(F32), w1, w2, r1, r2)
    meta = jnp.zeros(logits.shape, F32)
    for idx, val in enumerate(fields):
        meta = jnp.where(lane == idx, val, meta)
    meta_ref[...] = meta


def _router(h, w_pad, b_pad, experts):
    m, k = h.shape
    bm = _tile(m, ROW_TILE, SUBLANES)
    return pl.pallas_call(
        functools.partial(_router_kernel, experts=experts),
        grid=(m // bm,),
        in_specs=[pl.BlockSpec((bm, k), lambda i: (i, 0)),
                  pl.BlockSpec((k, LANES), lambda i: (0, 0)),
                  pl.BlockSpec((1, LANES), lambda i: (0, 0))],
        out_specs=[pl.BlockSpec((bm, LANES), lambda i: (i, 0)),
                   pl.BlockSpec((1, LANES), lambda i: (0, 0))],
        out_shape=[jax.ShapeDtypeStruct((m, LANES), F32), jax.ShapeDtypeStruct((1, LANES), F32)],
        scratch_shapes=[pltpu.VMEM((1, LANES), F32)],
        compiler_params=_params("arbitrary"),
        name="moe_router",
    )(h, w_pad, b_pad)


def _row_copy(src_ref, src_row, dst_ref, dst_row, sem):
    return pltpu.make_async_copy(src_ref.at[pl.ds(src_row, 1)], dst_ref.at[pl.ds(dst_row, 1)], sem)


def _dispatch_kernel(valid_end_ref, pad_end_ref, pos1_ref, pos2_ref, h_ref, xs_ref, zero_ref, sem, zero_sem,
                     *, rows, experts):
    @pl.when(pl.program_id(0) == 0)
    def _():
        zero_ref[...] = jnp.zeros_like(zero_ref)
        for e in range(experts):
            lo, hi = valid_end_ref[e], pad_end_ref[e]

            def fill(p, carry):
                _row_copy(zero_ref, 0, xs_ref, p, zero_sem).start()
                return carry

            def fill_wait(p, carry):
                _row_copy(zero_ref, 0, xs_ref, p, zero_sem).wait()
                return carry

            lax.fori_loop(lo, hi, fill, 0)
            lax.fori_loop(lo, hi, fill_wait, 0)

        zr = zero_ref.shape[0]
        tail = pltpu.make_async_copy
        first = pad_end_ref[experts - 1] // zr
        last = xs_ref.shape[0] // zr

        def tail_fill(b, carry):
            tail(zero_ref, xs_ref.at[pl.ds(pl.multiple_of(b * zr, zr), zr)], zero_sem).start()
            return carry

        def tail_wait(b, carry):
            tail(zero_ref, xs_ref.at[pl.ds(pl.multiple_of(b * zr, zr), zr)], zero_sem).wait()
            return carry

        lax.fori_loop(first, last, tail_fill, 0)
        lax.fori_loop(first, last, tail_wait, 0)

    def issue(r, carry):
        _row_copy(h_ref, r, xs_ref, pos1_ref[0, 0, r], sem).start()
        _row_copy(h_ref, r, xs_ref, pos2_ref[0, 0, r], sem).start()
        return carry

    def drain(r, carry):
        _row_copy(h_ref, r, xs_ref, pos1_ref[0, 0, r], sem).wait()
        _row_copy(h_ref, r, xs_ref, pos2_ref[0, 0, r], sem).wait()
        return carry

    lax.fori_loop(0, rows, issue, 0)
    lax.fori_loop(0, rows, drain, 0)


def _dispatch(h, pos1, pos2, valid_end, pad_end, total_rows, bm):
    n, d = h.shape
    rows = _tile(n, GATHER_ROWS, SUBLANES)
    steps = n // rows
    experts = valid_end.shape[0]
    zero_rows = _tile(bm, ZERO_ROWS, SUBLANES)
    pos_spec = pl.BlockSpec((1, 1, rows), lambda i, ve, pe: (i, 0, 0), memory_space=pltpu.SMEM)
    return pl.pallas_call(
        functools.partial(_dispatch_kernel, rows=rows, experts=experts),
        grid_spec=pltpu.PrefetchScalarGridSpec(
            num_scalar_prefetch=2,
            grid=(steps,),
            in_specs=[pos_spec, pos_spec, pl.BlockSpec((rows, d), lambda i, ve, pe: (i, 0))],
            out_specs=pl.BlockSpec(memory_space=pl.ANY),
            scratch_shapes=[pltpu.VMEM((zero_rows, d), F32), pltpu.SemaphoreType.DMA, pltpu.SemaphoreType.DMA],
        ),
        out_shape=jax.ShapeDtypeStruct((total_rows, d), F32),
        compiler_params=_params("arbitrary"),
        name="moe_dispatch",
    )(valid_end, pad_end, pos1.reshape(steps, 1, rows), pos2.reshape(steps, 1, rows), h)


def _moe_up_kernel(be_ref, nu_ref, a_ref, wg_ref, wu_ref, o_ref):
    used = pl.program_id(0) < nu_ref[0]

    @pl.when(used)
    def _():
        a = a_ref[...].astype(BF16)
        z1 = _dot(a, wg_ref[0])
        z2 = _dot(a, wu_ref[0])
        o_ref[...] = ((z1 * _sigmoid(z1)) * z2).astype(o_ref.dtype)

    @pl.when(jnp.logical_not(used))
    def _():
        o_ref[...] = jnp.zeros_like(o_ref)


def _moe_down_kernel(be_ref, nu_ref, g_ref, w_ref, o_ref):
    used = pl.program_id(0) < nu_ref[0]

    @pl.when(used)
    def _():
        o_ref[...] = _dot(g_ref[...], w_ref[0])

    @pl.when(jnp.logical_not(used))
    def _():
        o_ref[...] = jnp.zeros_like(o_ref)


def _moe_experts(xs, block_expert, n_used, w_gate, w_up, w_down, bm):
    p, d = xs.shape
    f = w_gate.shape[2]
    nblk = p // bm

    def row_idx(i, nu):
        return jnp.maximum(jnp.minimum(i, nu[0] - 1), 0)

    bn = _tile(f, COL_TILE_FUSED, LANES)
    nj = f // bn
    w_spec = pl.BlockSpec((1, d, bn), lambda i, j, be, nu: (be[i], 0, jnp.where(i < nu[0], j, nj - 1)))
    g = pl.pallas_call(
        _moe_up_kernel,
        grid_spec=pltpu.PrefetchScalarGridSpec(
            num_scalar_prefetch=2,
            grid=(nblk, nj),
            in_specs=[pl.BlockSpec((bm, d), lambda i, j, be, nu: (row_idx(i, nu), 0)), w_spec, w_spec],
            out_specs=pl.BlockSpec((bm, bn), lambda i, j, be, nu: (i, j)),
        ),
        out_shape=jax.ShapeDtypeStruct((p, f), BF16),
        compiler_params=_params("parallel", "arbitrary"),
        name="moe_up",
    )(block_expert, n_used, xs, w_gate, w_up)

    bn2 = _tile(d, MOE_DOWN_COLS, LANES)
    nj2 = d // bn2
    return pl.pallas_call(
        _moe_down_kernel,
        grid_spec=pltpu.PrefetchScalarGridSpec(
            num_scalar_prefetch=2,
            grid=(nblk, nj2),
            in_specs=[pl.BlockSpec((bm, f), lambda i, j, be, nu: (row_idx(i, nu), 0)),
                      pl.BlockSpec((1, f, bn2), lambda i, j, be, nu: (be[i], 0, jnp.where(i < nu[0], j, nj2 - 1)))],
            out_specs=pl.BlockSpec((bm, bn2), lambda i, j, be, nu: (i, j)),
        ),
        out_shape=jax.ShapeDtypeStruct((p, d), F32),
        compiler_params=_params("parallel", "arbitrary"),
        name="moe_down",
    )(block_expert, n_used, g, w_down)


def _combine_kernel(pos1_ref, pos2_ref, nxt1_ref, nxt2_ref, x_ref, meta_ref, gate_ref, y_ref, o_ref, buf_ref, sem,
                    *, rows, steps):
    i = pl.program_id(0)
    slot = i % 2

    def gather(p1_ref, p2_ref, s):
        def body(r, carry):
            _row_copy(y_ref, p1_ref[0, 0, r], buf_ref.at[s, 0], r, sem.at[s]).start()
            _row_copy(y_ref, p2_ref[0, 0, r], buf_ref.at[s, 1], r, sem.at[s]).start()
            return carry
        lax.fori_loop(0, rows, body, 0)

    @pl.when(i == 0)
    def _():
        gather(pos1_ref, pos2_ref, 0)

    @pl.when(i + 1 < steps)
    def _():
        gather(nxt1_ref, nxt2_ref, 1 - slot)

    def drain(r, carry):
        _row_copy(y_ref, 0, buf_ref.at[slot, 0], r, sem.at[slot]).wait()
        _row_copy(y_ref, 0, buf_ref.at[slot, 1], r, sem.at[slot]).wait()
        return carry

    lax.fori_loop(0, rows, drain, 0)
    w1 = meta_ref[:, 2:3]
    w2 = meta_ref[:, 3:4]
    o_ref[...] = x_ref[...] + gate_ref[0] * (w1 * buf_ref[slot, 0] + w2 * buf_ref[slot, 1])


def _combine(x2, y, pos1, pos2, meta, gate, seq):
    n, d = x2.shape
    rows = _tile(seq, GATHER_ROWS, SUBLANES)
    steps = n // rows
    pos_spec = pl.BlockSpec((1, 1, rows), lambda i: (i, 0, 0), memory_space=pltpu.SMEM)
    nxt_spec = pl.BlockSpec((1, 1, rows), lambda i: (jnp.minimum(i + 1, steps - 1), 0, 0), memory_space=pltpu.SMEM)
    p1 = pos1.reshape(steps, 1, rows)
    p2 = pos2.reshape(steps, 1, rows)
    return pl.pallas_call(
        functools.partial(_combine_kernel, rows=rows, steps=steps),
        grid=(steps,),
        in_specs=[pos_spec, pos_spec, nxt_spec, nxt_spec,
                  pl.BlockSpec((rows, d), lambda i: (i, 0)),
                  pl.BlockSpec((rows, LANES), lambda i: (i, 0)),
                  pl.BlockSpec((1, 1, d), lambda i: (i * rows // seq, 0, 0)),
                  pl.BlockSpec(memory_space=pl.ANY)],
        out_specs=pl.BlockSpec((rows, d), lambda i: (i, 0)),
        out_shape=jax.ShapeDtypeStruct((n, d), F32),
        scratch_shapes=[pltpu.VMEM((2, 2, rows, d), F32), pltpu.SemaphoreType.DMA((2,))],
        compiler_params=_params("arbitrary"),
        name="moe_combine",
    )(p1, p2, p1, p2, x2, meta, gate, y)


def _moe_layer(x2, h, gate, seq, w_router, b_router, w_gate, w_up, w_down):
    n, d = h.shape
    experts = w_router.shape[1]
    meta, counts = _router(h, _pad_cols(w_router, LANES), _pad_cols(b_router.reshape(1, experts), LANES), experts)
    bm = _tile(2 * n, MOE_ROWS, SUBLANES)
    total_rows = 2 * n + experts * bm
    i1, i2 = meta[:, 0].astype(jnp.int32), meta[:, 1].astype(jnp.int32)
    r1, r2 = meta[:, 4].astype(jnp.int32), meta[:, 5].astype(jnp.int32)
    cnt = counts[0, :experts].astype(jnp.int32)
    padded = (cnt + bm - 1) // bm * bm
    pad_end = jnp.cumsum(padded)
    start = pad_end - padded
    valid_end = start + cnt
    expert_ids = jnp.arange(experts, dtype=jnp.int32)[None, :]
    start_of = lambda idx: jnp.sum(jnp.where(idx[:, None] == expert_ids, start[None, :], 0), axis=1)
    pos1 = start_of(i1) + r1
    pos2 = start_of(i2) + r2
    blk_start = jnp.arange(total_rows // bm, dtype=jnp.int32) * bm
    block_expert = jnp.minimum(jnp.sum(blk_start[:, None] >= pad_end[None, :], axis=1), experts - 1).astype(jnp.int32)
    n_used = (pad_end[-1:] // bm).astype(jnp.int32)

    xs = _dispatch(h, pos1, pos2, valid_end, pad_end, total_rows, bm)
    y = _moe_experts(xs, block_expert, n_used, w_gate.astype(BF16), w_up.astype(BF16), w_down.astype(BF16), bm)
    return _combine(x2, y, pos1, pos2, meta, gate, seq)


def kernel(x, c, cond_w, cond_b, ada_w, ada_b, mix_norm_g, ffn_norm_g, final_norm_g, conv_w_in, conv_b_in, conv_w_dw, conv_b_dw, conv_ln_g, conv_ln_b, conv_w_out, conv_b_out, mlstm_w_in, mlstm_b_gates, mlstm_norm_g, mlstm_w_out, ffn_w_gate, ffn_w_up, ffn_w_down, moe_w_router, moe_b_router, moe_w_gate, moe_w_up, moe_w_down):
    batch, seq, d = x.shape
    n = batch * seq
    depth = ada_w.shape[0]
    heads = mlstm_b_gates.shape[1] // 2
    dv = mlstm_norm_g.shape[1] // heads
    dk = (mlstm_w_in.shape[2] - 2 * heads * dv - 2 * heads) // (2 * heads)
    experts = moe_w_router.shape[2]
    assert dk % LANES == 0 and dv % LANES == 0 and 2 * heads <= LANES and experts <= LANES

    c_pad = jnp.pad(c, ((0, SUBLANES - batch % SUBLANES if batch % SUBLANES else 0), (0, 0)))
    e = _cond_embed(c_pad, cond_w, cond_b)
    mods = _cond_mod(e, ada_w, ada_b)[:, :batch].reshape(depth, batch, 6, 1, d)

    xf = x.reshape(n, d)
    conv_bias_in = conv_b_in.reshape(-1, 1, 2 * d)
    conv_bias_out = conv_b_out.reshape(-1, 1, d)
    ffn_down_bf16 = ffn_w_down.astype(BF16)
    n_main = 2 * heads * (dk + dv)
    for i in range(depth):
        j = i // 2
        shift1, scale1, gate1, shift2, scale2, gate2 = [mods[i, :, t] for t in range(6)]

        h = _norm(xf, mix_norm_g[i], seq, scale1, shift1)
        if i % 2 == 0:
            u = _mm_gated(h, conv_w_in, conv_w_in, j, d, act="glu", out_dtype=F32, off1=0, off2=d, bias=conv_bias_in)
            v = _conv_ln_silu(u.reshape(batch, seq, d), conv_w_dw[j], conv_b_dw[j], conv_ln_g[j], conv_ln_b[j])
            xf = _mm_ws(v.reshape(n, d), conv_w_out, j, d, out_dtype=F32, bias=conv_bias_out,
                        res=xf, gate=gate1, seq=seq)
        else:
            w_gates = _pad_cols(mlstm_w_in[j][:, n_main:], LANES).astype(BF16)
            b_gates = _pad_cols(mlstm_b_gates[j].reshape(1, 2 * heads), LANES)
            proj = _mm_ws(h, mlstm_w_in, j, n_main, out_dtype=BF16)
            gates = _mlstm_gates(h, w_gates, b_gates, heads)[:, :2 * heads]
            chunk = _tile(seq, MLSTM_CHUNK, LANES)
            gates_t = gates.reshape(batch, seq // chunk, chunk, 2 * heads).transpose(0, 3, 1, 2)
            y = _mlstm_cell(proj, gates_t[:, :, :, None, :], gates_t[:, :, :, :, None],
                            mlstm_norm_g[j], batch, seq, heads, dk, dv)
            xf = _mm_ws(y, mlstm_w_out, j, d, out_dtype=F32, res=xf, gate=gate1, seq=seq)

        h = _norm(xf, ffn_norm_g[i], seq, scale2, shift2, out_dtype=BF16 if i % 2 == 0 else F32)
        if i % 2 == 0:
            g = _mm_gated(h, ffn_w_gate, ffn_w_up, j, ffn_w_gate.shape[2], act="swiglu", out_dtype=BF16)
            xf = _mm_long(g, ffn_down_bf16, j, res=xf, gate=gate2, seq=seq)
        else:
            xf = _moe_layer(xf, h, gate2, seq, moe_w_router[j], moe_b_router[j],
                            moe_w_gate[j], moe_w_up[j], moe_w_down[j])

    out = _norm(xf, final_norm_g, seq, out_dtype=F32)
    return out.reshape(batch, seq, d)
```

```python
import functools

import jax
import jax.numpy as jnp
from jax import lax
from jax.experimental import pallas as pl
from jax.experimental.pallas import tpu as pltpu

EPS = 1e-6
F32 = jnp.float32
BF16 = jnp.bfloat16

LANES = 128
SUBLANES = 8
VMEM_LIMIT_BYTES = 56 * 1024 * 1024

ROW_TILE = 1024
COL_TILE_FUSED = 512
COL_TILE_WIDE = 1024
LONG_K_ROWS = 512
MOE_DOWN_COLS = 2048
LN_ROWS = 16
LN_UNROLL = 4
NORM_ROWS = 256
CONV_ROWS = 256
CONV_HALO = 32
CONV_GROUP = 64
MLSTM_CHUNK = 256
MOE_ROWS = 512
GATHER_ROWS = 256
ZERO_ROWS = 64


def _tile(dim, preferred, align):
    if dim <= preferred:
        return dim
    t = preferred - preferred % align
    while t > align and dim % t:
        t -= align
    assert dim % t == 0, (dim, preferred, align)
    return t


def _params(*sem):
    return pltpu.CompilerParams(dimension_semantics=sem, vmem_limit_bytes=VMEM_LIMIT_BYTES)


def _sigmoid(z):
    return 1.0 / (1.0 + jnp.exp(-z))


def _dot(a, b):
    return jnp.dot(a, b, preferred_element_type=F32)


def _dot_nt(a, b_t):
    return lax.dot_general(a, b_t, (((1,), (1,)), ((), ())), preferred_element_type=F32)


def _cond_embed_kernel(c_ref, w_ref, b_ref, o_ref):
    z = jnp.dot(c_ref[...], w_ref[...], preferred_element_type=F32,
                precision=lax.Precision.HIGHEST) + b_ref[...]
    o_ref[...] = z * _sigmoid(z)


def _cond_embed(c_pad, cond_w, cond_b):
    rows, r = c_pad.shape[0], cond_w.shape[1]
    return pl.pallas_call(
        _cond_embed_kernel,
        out_shape=jax.ShapeDtypeStruct((rows, r), F32),
        compiler_params=pltpu.CompilerParams(vmem_limit_bytes=VMEM_LIMIT_BYTES),
        name="cond_embed",
    )(c_pad, cond_w, cond_b.reshape(1, r))


def _cond_mod_kernel(e_ref, w_ref, b_ref, o_ref):
    o_ref[0] = jnp.dot(e_ref[...], w_ref[0], preferred_element_type=F32,
                       precision=lax.Precision.HIGHEST) + b_ref[0]


def _cond_mod(e, ada_w, ada_b):
    depth, r, width = ada_w.shape
    rows = e.shape[0]
    bn = _tile(width, 2048, LANES)
    return pl.pallas_call(
        _cond_mod_kernel,
        grid=(depth, width // bn),
        in_specs=[
            pl.BlockSpec((rows, r), lambda i, j: (0, 0)),
            pl.BlockSpec((1, r, bn), lambda i, j: (i, 0, j)),
            pl.BlockSpec((1, 1, bn), lambda i, j: (i, 0, j)),
        ],
        out_specs=pl.BlockSpec((1, rows, bn), lambda i, j: (i, 0, j)),
        out_shape=jax.ShapeDtypeStruct((depth, rows, width), F32),
        compiler_params=_params("parallel", "parallel"),
        name="cond_mod",
    )(e, ada_w, ada_b.reshape(depth, 1, width))


def _norm_kernel(x_ref, g_ref, *rest, modulate):
    if modulate:
        sc_ref, sh_ref, o_ref = rest
    else:
        (o_ref,) = rest
    x = x_ref[...]
    y = x * lax.rsqrt(jnp.mean(x * x, axis=-1, keepdims=True) + EPS) * g_ref[...]
    if modulate:
        y = y * (1.0 + sc_ref[0]) + sh_ref[0]
    o_ref[...] = y.astype(o_ref.dtype)


def _norm(x2, g, seq, scale=None, shift=None, out_dtype=BF16):
    n, d = x2.shape
    tm = _tile(seq, NORM_ROWS, SUBLANES)
    modulate = scale is not None
    in_specs = [pl.BlockSpec((tm, d), lambda i: (i, 0)),
                pl.BlockSpec((1, d), lambda i: (0, 0))]
    args = [x2, g.reshape(1, d)]
    if modulate:
        per_batch = pl.BlockSpec((1, 1, d), lambda i: (i * tm // seq, 0, 0))
        in_specs += [per_batch, per_batch]
        args += [scale, shift]
    return pl.pallas_call(
        functools.partial(_norm_kernel, modulate=modulate),
        grid=(n // tm,),
        in_specs=in_specs,
        out_specs=pl.BlockSpec((tm, d), lambda i: (i, 0)),
        out_shape=jax.ShapeDtypeStruct((n, d), out_dtype),
        compiler_params=_params("parallel"),
        name="norm_mod" if modulate else "rms_norm",
    )(*args)


def _stage_weights(w_hbm, layer, col_off, stage_ref, wb_ref, sem, *, bn, nj, transposed=False):
    j = pl.program_id(0)
    first_row_block = pl.program_id(1) == 0

    def fetch(jj):
        cols = pl.ds(pl.multiple_of((jj + col_off) * bn, bn), bn)
        src = w_hbm.at[layer, cols, :] if transposed else w_hbm.at[layer, :, cols]
        return pltpu.make_async_copy(src, stage_ref, sem)

    @pl.when(jnp.logical_and(first_row_block, j == 0))
    def _():
        fetch(0).start()

    @pl.when(first_row_block)
    def _():
        fetch(j).wait()
        wb_ref[...] = stage_ref[...].astype(BF16)

        @pl.when(j + 1 < nj)
        def _():
            fetch(j + 1).start()


def _mm_gated_kernel(a_ref, w1_hbm, w2_hbm, *rest, act, has_bias, layer, o1, o2, bn, nj):
    rest = list(rest)
    sem = rest.pop()
    w2b_ref = rest.pop()
    w1b_ref = rest.pop()
    stage2_ref = rest.pop()
    stage1_ref = rest.pop()
    o_ref = rest.pop()
    _stage_weights(w1_hbm, layer, o1, stage1_ref, w1b_ref, sem.at[0], bn=bn, nj=nj)
    _stage_weights(w2_hbm, layer, o2, stage2_ref, w2b_ref, sem.at[1], bn=bn, nj=nj)

    a = a_ref[...]
    z1 = _dot(a, w1b_ref[...])
    z2 = _dot(a, w2b_ref[...])
    if has_bias:
        z1 = z1 + rest[0][0]
        z2 = z2 + rest[1][0]
    if act == "glu":
        y = z1 * _sigmoid(z2)
    else:
        y = (z1 * _sigmoid(z1)) * z2
    o_ref[...] = y.astype(o_ref.dtype)


def _mm_gated(a, w1, w2, layer, n_out, *, act, out_dtype, off1=0, off2=0, bias=None):
    m, k = a.shape
    bm = _tile(m, ROW_TILE, SUBLANES)
    bn = _tile(n_out, COL_TILE_FUSED, LANES)
    assert off1 % bn == 0 and off2 % bn == 0
    o1, o2 = off1 // bn, off2 // bn
    nj = n_out // bn
    in_specs = [
        pl.BlockSpec((bm, k), lambda j, i: (i, 0)),
        pl.BlockSpec(memory_space=pl.ANY),
        pl.BlockSpec(memory_space=pl.ANY),
    ]
    args = [a, w1, w2]
    if bias is not None:
        in_specs += [pl.BlockSpec((1, 1, bn), lambda j, i: (layer, 0, j + o1)),
                     pl.BlockSpec((1, 1, bn), lambda j, i: (layer, 0, j + o2))]
        args += [bias, bias]
    return pl.pallas_call(
        functools.partial(_mm_gated_kernel, act=act, has_bias=bias is not None,
                          layer=layer, o1=o1, o2=o2, bn=bn, nj=nj),
        grid=(nj, m // bm),
        in_specs=in_specs,
        out_specs=pl.BlockSpec((bm, bn), lambda j, i: (i, j)),
        out_shape=jax.ShapeDtypeStruct((m, n_out), out_dtype),
        scratch_shapes=[pltpu.VMEM((k, bn), F32), pltpu.VMEM((k, bn), F32),
                        pltpu.VMEM((k, bn), BF16), pltpu.VMEM((k, bn), BF16), pltpu.SemaphoreType.DMA((2,))],
        compiler_params=_params("arbitrary", "arbitrary"),
        name="mm_" + act,
    )(*args)


def _mm_ws_kernel(a_ref, w_hbm, *rest, has_bias, has_res, layer, bn, nj, transposed):
    rest = list(rest)
    sem = rest.pop()
    wb_ref = rest.pop()
    stage_ref = rest.pop()
    o_ref = rest.pop()
    _stage_weights(w_hbm, layer, 0, stage_ref, wb_ref, sem, bn=bn, nj=nj, transposed=transposed)

    acc = _dot_nt(a_ref[...], wb_ref[...]) if transposed else _dot(a_ref[...], wb_ref[...])
    idx = 0
    if has_bias:
        acc = acc + rest[idx][0]
        idx += 1
    if has_res:
        acc = rest[idx][...] + rest[idx + 1][0] * acc
    o_ref[...] = acc.astype(o_ref.dtype)


def _mm_ws(a, w, layer, n_out, *, out_dtype, bias=None, res=None, gate=None, seq=None, transposed=False):
    m, k = a.shape
    bm = _tile(m if seq is None else seq, ROW_TILE, SUBLANES)
    bn = _tile(n_out, COL_TILE_FUSED if res is not None else COL_TILE_WIDE, LANES)
    nj = n_out // bn
    w_block = (bn, k) if transposed else (k, bn)
    in_specs = [pl.BlockSpec((bm, k), lambda j, i: (i, 0)),
                pl.BlockSpec(memory_space=pl.ANY)]
    args = [a, w]
    if bias is not None:
        in_specs.append(pl.BlockSpec((1, 1, bn), lambda j, i: (layer, 0, j)))
        args.append(bias)
    if res is not None:
        in_specs += [pl.BlockSpec((bm, bn), lambda j, i: (i, j)),
                     pl.BlockSpec((1, 1, bn), lambda j, i: (i * bm // seq, 0, j))]
        args += [res, gate]
    return pl.pallas_call(
        functools.partial(_mm_ws_kernel, has_bias=bias is not None, has_res=res is not None,
                          layer=layer, bn=bn, nj=nj, transposed=transposed),
        grid=(nj, m // bm),
        in_specs=in_specs,
        out_specs=pl.BlockSpec((bm, bn), lambda j, i: (i, j)),
        out_shape=jax.ShapeDtypeStruct((m, n_out), out_dtype),
        scratch_shapes=[pltpu.VMEM(w_block, F32), pltpu.VMEM(w_block, BF16), pltpu.SemaphoreType.DMA],
        compiler_params=_params("arbitrary", "arbitrary"),
        name="mm_res" if res is not None else "mm",
    )(*args)


def _mm_long_kernel(a_ref, w_ref, res_ref, gate_ref, o_ref):
    o_ref[...] = res_ref[...] + gate_ref[0] * _dot(a_ref[...], w_ref[0])


def _mm_long(a, w, layer, *, res, gate, seq):
    m, k = a.shape
    n_out = w.shape[2]
    bm = _tile(seq, LONG_K_ROWS, SUBLANES)
    bn = _tile(n_out, COL_TILE_FUSED, LANES)
    return pl.pallas_call(
        _mm_long_kernel,
        grid=(m // bm, n_out // bn),
        in_specs=[pl.BlockSpec((bm, k), lambda i, j: (i, 0)),
                  pl.BlockSpec((1, k, bn), lambda i, j: (layer, 0, j)),
                  pl.BlockSpec((bm, bn), lambda i, j: (i, j)),
                  pl.BlockSpec((1, 1, bn), lambda i, j: (i * bm // seq, 0, j))],
        out_specs=pl.BlockSpec((bm, bn), lambda i, j: (i, j)),
        out_shape=jax.ShapeDtypeStruct((m, n_out), F32),
        compiler_params=_params("parallel", "arbitrary"),
        name="mm_long",
    )(a, w, res, gate)


def _conv_kernel(u_ref, halo_ref, w_ref, bdw_ref, g_ref, b_ref, o_ref, buf_ref, y_ref, *, ts, kw, lc):
    d = y_ref.shape[1]
    s = pl.program_id(1)

    @pl.when(s == 0)
    def _():
        buf_ref[0:CONV_HALO, :] = jnp.zeros((CONV_HALO, d), F32)

    @pl.when(s > 0)
    def _():
        buf_ref[0:CONV_HALO, :] = halo_ref[0]

    buf_ref[CONV_HALO:, :] = u_ref[0]
    base = CONV_HALO - (kw - 1)

    def lane_chunk(c, carry):
        l0 = pl.multiple_of(c * lc, lc)
        lanes = pl.ds(l0, lc)
        bias = bdw_ref[:, lanes]
        win_rows = CONV_GROUP + CONV_HALO
        for g0 in range(0, ts, CONV_GROUP):
            win = buf_ref[pl.ds(g0, win_rows), lanes]
            acc = jnp.broadcast_to(bias, (CONV_GROUP, lc))
            for r in range(SUBLANES):
                shifted = win if r == 0 else pltpu.roll(win, win_rows - r, axis=0)
                for a in range(CONV_HALO // SUBLANES + 1):
                    j = SUBLANES * a + r
                    if base <= j < base + kw:
                        acc = acc + w_ref[j - base:j - base + 1, lanes] * shifted[SUBLANES * a:SUBLANES * a + CONV_GROUP]
            y_ref[pl.ds(g0, CONV_GROUP), lanes] = acc
        return carry

    lax.fori_loop(0, d // lc, lane_chunk, 0)

    def layer_norm_rows(t, carry):
        rows = pl.ds(pl.multiple_of(t * LN_ROWS, LN_ROWS), LN_ROWS)
        y = y_ref[rows, :]
        mu = jnp.mean(y, axis=-1, keepdims=True)
        yc = y - mu
        var = jnp.mean(yc * yc, axis=-1, keepdims=True)
        z = yc * lax.rsqrt(var + EPS) * g_ref[...] + b_ref[...]
        o_ref[0, rows, :] = (z * _sigmoid(z)).astype(o_ref.dtype)
        return carry

    lax.fori_loop(0, ts // LN_ROWS, layer_norm_rows, 0, unroll=LN_UNROLL)


def _conv_ln_silu(u3, w_dw, b_dw, ln_g, ln_b):
    b, s, d = u3.shape
    kw = w_dw.shape[0]
    assert kw - 1 <= CONV_HALO
    ts = _tile(s, CONV_ROWS, CONV_GROUP)
    assert ts % LN_ROWS == 0
    halo_blocks = ts // CONV_HALO
    vec = pl.BlockSpec((1, d), lambda bi, si: (0, 0))
    return pl.pallas_call(
        functools.partial(_conv_kernel, ts=ts, kw=kw, lc=LANES),
        grid=(b, s // ts),
        in_specs=[
            pl.BlockSpec((1, ts, d), lambda bi, si: (bi, si, 0)),
            pl.BlockSpec((1, CONV_HALO, d), lambda bi, si: (bi, jnp.maximum(si * halo_blocks - 1, 0), 0)),
            pl.BlockSpec((kw, d), lambda bi, si: (0, 0)),
            vec, vec, vec,
        ],
        out_specs=pl.BlockSpec((1, ts, d), lambda bi, si: (bi, si, 0)),
        out_shape=jax.ShapeDtypeStruct((b, s, d), BF16),
        scratch_shapes=[pltpu.VMEM((ts + CONV_HALO, d), F32), pltpu.VMEM((ts, d), F32)],
        compiler_params=_params("parallel", "arbitrary"),
        name="dwconv_ln_silu",
    )(u3, u3, w_dw, b_dw.reshape(1, d), ln_g.reshape(1, d), ln_b.reshape(1, d))


def _gates_kernel(a_ref, w_ref, b_ref, o_ref, *, heads):
    z = _dot_nt(a_ref[...], w_ref[...].astype(BF16)) + b_ref[...]
    lane = lax.broadcasted_iota(jnp.int32, z.shape, 1)
    log_sig = -(jnp.maximum(-z, 0.0) + jnp.log1p(jnp.exp(-jnp.abs(z))))
    o_ref[...] = jnp.where((lane >= heads) & (lane < 2 * heads), log_sig, z)


def _mlstm_gates(h, w_pad_t, b_pad, heads):
    m, k = h.shape
    bm = _tile(m, ROW_TILE, SUBLANES)
    return pl.pallas_call(
        functools.partial(_gates_kernel, heads=heads),
        grid=(m // bm,),
        in_specs=[pl.BlockSpec((bm, k), lambda i: (i, 0)),
                  pl.BlockSpec((LANES, k), lambda i: (0, 0)),
                  pl.BlockSpec((1, LANES), lambda i: (0, 0))],
        out_specs=pl.BlockSpec((bm, LANES), lambda i: (i, 0)),
        out_shape=jax.ShapeDtypeStruct((m, LANES), F32),
        compiler_params=_params("parallel"),
        name="mlstm_gates",
    )(h, w_pad_t, b_pad)


def _mlstm_kernel(q_ref, k_ref, v_ref, o_ref, igr_ref, lfr_ref, ng_ref,
                  out_ref, c_ref, n_ref, m_ref, *, chunk, dk):
    @pl.when(pl.program_id(2) == 0)
    def _():
        c_ref[...] = jnp.zeros_like(c_ref)
        n_ref[...] = jnp.zeros_like(n_ref)
        m_ref[...] = jnp.zeros_like(m_ref)

    q = q_ref[...]
    k = k_ref[...]
    v = v_ref[...]
    ig_r, lf_r = igr_ref[...], lfr_ref[...]
    row = lax.broadcasted_iota(jnp.int32, (chunk, chunk), 0)
    col = lax.broadcasted_iota(jnp.int32, (chunk, chunk), 1)
    causal = col <= row
    diag = row == col
    ig_c = jnp.sum(jnp.where(diag, ig_r, 0.0), axis=1, keepdims=True)
    lf_c = jnp.sum(jnp.where(diag, lf_r, 0.0), axis=1, keepdims=True)
    b_c = jnp.sum(jnp.where(causal, lf_r, 0.0), axis=1, keepdims=True)
    b_r = jnp.sum(jnp.where(row <= col, lf_c, 0.0), axis=0, keepdims=True)
    b_end = jnp.sum(lf_r, axis=1, keepdims=True)
    m_prev = m_ref[...]
    a_c = b_c + m_prev
    dmat = jnp.where(causal, b_c - b_r + ig_r, -jnp.inf)
    m_s = jnp.maximum(a_c, jnp.max(dmat, axis=1, keepdims=True))
    w_inter = jnp.exp(a_c - m_s)
    scale = dk ** -0.5
    p = jnp.exp(dmat - m_s) * scale
    s = lax.dot_general(q, k, (((1,), (1,)), ((), ())), preferred_element_type=F32) * p
    c_state = c_ref[...]
    n_state = n_ref[...]
    num = _dot(s.astype(BF16), v) + w_inter * _dot(q, c_state.astype(BF16))
    den = (jnp.sum(s, axis=1, keepdims=True)
           + w_inter * jnp.sum(q.astype(F32) * n_state, axis=1, keepdims=True))
    h = num / jnp.maximum(jnp.abs(den), jnp.exp(-m_s))
    hn = h * lax.rsqrt(jnp.mean(h * h, axis=1, keepdims=True) + EPS) * ng_ref[...]
    out_ref[...] = (hn * _sigmoid(o_ref[...].astype(F32))).astype(out_ref.dtype)

    g_c = b_end - b_c + ig_c
    m_new = jnp.maximum(b_end + m_prev, jnp.max(g_c, axis=0, keepdims=True))
    decay = jnp.exp(b_end + m_prev - m_new)
    wk = k.astype(F32) * (jnp.exp(g_c - m_new) * scale)
    c_ref[...] = decay * c_state + lax.dot_general(
        wk.astype(BF16), v, (((0,), (0,)), ((), ())), preferred_element_type=F32)
    n_ref[...] = decay * n_state + jnp.sum(wk, axis=0, keepdims=True)
    m_ref[...] = m_new


def _mlstm_cell(proj, gates_rows, norm_g, batch, seq, heads, dk, dv):
    n = proj.shape[0]
    chunk = _tile(seq, MLSTM_CHUNK, LANES)
    nc = seq // chunk
    v_off = 2 * heads * dk // dv
    row_blk = lambda b, h, c: b * nc + c
    g_row = lambda off: pl.BlockSpec((None, None, None, 1, chunk), lambda b, h, c: (b, off + h, c, 0, 0))
    return pl.pallas_call(
        functools.partial(_mlstm_kernel, chunk=chunk, dk=dk),
        grid=(batch, heads, nc),
        in_specs=[
            pl.BlockSpec((chunk, dk), lambda b, h, c: (row_blk(b, h, c), h)),
            pl.BlockSpec((chunk, dk), lambda b, h, c: (row_blk(b, h, c), heads + h)),
            pl.BlockSpec((chunk, dv), lambda b, h, c: (row_blk(b, h, c), v_off + h)),
            pl.BlockSpec((chunk, dv), lambda b, h, c: (row_blk(b, h, c), v_off + heads + h)),
            g_row(0), g_row(heads),
            pl.BlockSpec((1, dv), lambda b, h, c: (0, h)),
        ],
        out_specs=pl.BlockSpec((chunk, dv), lambda b, h, c: (row_blk(b, h, c), h)),
        out_shape=jax.ShapeDtypeStruct((n, heads * dv), BF16),
        scratch_shapes=[pltpu.VMEM((dk, dv), F32), pltpu.VMEM((1, dk), F32), pltpu.VMEM((1, 1), F32)],
        compiler_params=_params("parallel", "parallel", "arbitrary"),
        name="mlstm_cell",
    )(proj, proj, proj, proj, gates_rows, gates_rows, norm_g.reshape(1, heads * dv))


def _pad_cols(w, width):
    return jnp.pad(w, ((0, 0), (0, width - w.shape[1])))


def _router_kernel(a_ref, w_ref, b_ref, meta_ref, cnt_ref, carry_ref, *, experts):
    @pl.when(pl.program_id(0) == 0)
    def _():
        carry_ref[...] = jnp.zeros_like(carry_ref)

    logits = jnp.dot(a_ref[...], w_ref[...], preferred_element_type=F32,
                     precision=lax.Precision.HIGHEST) + b_ref[...]
    rows = logits.shape[0]
    lane = lax.broadcasted_iota(jnp.int32, logits.shape, 1)
    neg = -jnp.inf
    logits = jnp.where(lane < experts, logits, neg)
    v1 = jnp.max(logits, axis=1, keepdims=True)
    i1 = jnp.min(jnp.where(logits == v1, lane, LANES), axis=1, keepdims=True)
    rest = jnp.where(lane == i1, neg, logits)
    v2 = jnp.max(rest, axis=1, keepdims=True)
    i2 = jnp.min(jnp.where(rest == v2, lane, LANES), axis=1, keepdims=True)
    e2 = jnp.exp(v2 - v1)
    w1 = 1.0 / (1.0 + e2)
    w2 = e2 / (1.0 + e2)

    oh1 = jnp.where(lane == i1, 1.0, 0.0)
    oh2 = jnp.where(lane == i2, 1.0, 0.0)
    oh = oh1 + oh2
    tr = lax.broadcasted_iota(jnp.int32, (rows, rows), 0)
    tc = lax.broadcasted_iota(jnp.int32, (rows, rows), 1)
    tri = jnp.where(tc < tr, 1.0, 0.0).astype(BF16)
    rank = _dot(tri, oh.astype(BF16)) + carry_ref[...]
    r1 = jnp.sum(oh1 * rank, axis=1, keepdims=True)
    r2 = jnp.sum(oh2 * rank, axis=1, keepdims=True)
    carry_ref[...] += jnp.sum(oh, axis=0, keepdims=True)
    cnt_ref[...] = carry_ref[...]

    fields = (i1.astype(F32), i2.astype(F32), w1, w2, r1, r2)
    meta = jnp.zeros(logits.shape, F32)
    for idx, val in enumerate(fields):
        meta = jnp.where(lane == idx, val, meta)
    meta_ref[...] = meta


def _router(h, w_pad, b_pad, experts):
    m, k = h.shape
    bm = _tile(m, ROW_TILE, SUBLANES)
    return pl.pallas_call(
        functools.partial(_router_kernel, experts=experts),
        grid=(m // bm,),
        in_specs=[pl.BlockSpec((bm, k), lambda i: (i, 0)),
                  pl.BlockSpec((k, LANES), lambda i: (0, 0)),
                  pl.BlockSpec((1, LANES), lambda i: (0, 0))],
        out_specs=[pl.BlockSpec((bm, LANES), lambda i: (i, 0)),
                   pl.BlockSpec((1, LANES), lambda i: (0, 0))],
        out_shape=[jax.ShapeDtypeStruct((m, LANES), F32), jax.ShapeDtypeStruct((1, LANES), F32)],
        scratch_shapes=[pltpu.VMEM((1, LANES), F32)],
        compiler_params=_params("arbitrary"),
        name="moe_router",
    )(h, w_pad, b_pad)


def _row_copy(src_ref, src_row, dst_ref, dst_row, sem):
    return pltpu.make_async_copy(src_ref.at[pl.ds(src_row, 1)], dst_ref.at[pl.ds(dst_row, 1)], sem)


def _dispatch_kernel(valid_end_ref, pad_end_ref, pos1_ref, pos2_ref, h_ref, xs_ref, zero_ref, sem, zero_sem,
                     *, rows, experts):
    @pl.when(pl.program_id(0) == 0)
    def _():
        zero_ref[...] = jnp.zeros_like(zero_ref)
        for e in range(experts):
            lo, hi = valid_end_ref[e], pad_end_ref[e]

            def fill(p, carry):
                _row_copy(zero_ref, 0, xs_ref, p, zero_sem).start()
                return carry

            def fill_wait(p, carry):
                _row_copy(zero_ref, 0, xs_ref, p, zero_sem).wait()
                return carry

            lax.fori_loop(lo, hi, fill, 0)
            lax.fori_loop(lo, hi, fill_wait, 0)

        zr = zero_ref.shape[0]
        tail = pltpu.make_async_copy
        first = pad_end_ref[experts - 1] // zr
        last = xs_ref.shape[0] // zr

        def tail_fill(b, carry):
            tail(zero_ref, xs_ref.at[pl.ds(pl.multiple_of(b * zr, zr), zr)], zero_sem).start()
            return carry

        def tail_wait(b, carry):
            tail(zero_ref, xs_ref.at[pl.ds(pl.multiple_of(b * zr, zr), zr)], zero_sem).wait()
            return carry

        lax.fori_loop(first, last, tail_fill, 0)
        lax.fori_loop(first, last, tail_wait, 0)

    def issue(r, carry):
        _row_copy(h_ref, r, xs_ref, pos1_ref[0, 0, r], sem).start()
        _row_copy(h_ref, r, xs_ref, pos2_ref[0, 0, r], sem).start()
        return carry

    def drain(r, carry):
        _row_copy(h_ref, r, xs_ref, pos1_ref[0, 0, r], sem).wait()
        _row_copy(h_ref, r, xs_ref, pos2_ref[0, 0, r], sem).wait()
        return carry

    lax.fori_loop(0, rows, issue, 0)
    lax.fori_loop(0, rows, drain, 0)


def _dispatch(h, pos1, pos2, valid_end, pad_end, total_rows, bm):
    n, d = h.shape
    rows = _tile(n, GATHER_ROWS, SUBLANES)
    steps = n // rows
    experts = valid_end.shape[0]
    zero_rows = _tile(bm, ZERO_ROWS, SUBLANES)
    pos_spec = pl.BlockSpec((1, 1, rows), lambda i, ve, pe: (i, 0, 0), memory_space=pltpu.SMEM)
    return pl.pallas_call(
        functools.partial(_dispatch_kernel, rows=rows, experts=experts),
        grid_spec=pltpu.PrefetchScalarGridSpec(
            num_scalar_prefetch=2,
            grid=(steps,),
            in_specs=[pos_spec, pos_spec, pl.BlockSpec((rows, d), lambda i, ve, pe: (i, 0))],
            out_specs=pl.BlockSpec(memory_space=pl.ANY),
            scratch_shapes=[pltpu.VMEM((zero_rows, d), F32), pltpu.SemaphoreType.DMA, pltpu.SemaphoreType.DMA],
        ),
        out_shape=jax.ShapeDtypeStruct((total_rows, d), F32),
        compiler_params=_params("arbitrary"),
        name="moe_dispatch",
    )(valid_end, pad_end, pos1.reshape(steps, 1, rows), pos2.reshape(steps, 1, rows), h)


def _moe_up_kernel(be_ref, nu_ref, a_ref, wg_ref, wu_ref, o_ref):
    used = pl.program_id(0) < nu_ref[0]

    @pl.when(used)
    def _():
        a = a_ref[...].astype(BF16)
        z1 = _dot(a, wg_ref[0, 0])
        z2 = _dot(a, wu_ref[0, 0])
        o_ref[...] = ((z1 * _sigmoid(z1)) * z2).astype(o_ref.dtype)

    @pl.when(jnp.logical_not(used))
    def _():
        o_ref[...] = jnp.zeros_like(o_ref)


def _moe_down_kernel(be_ref, nu_ref, g_ref, w_ref, o_ref):
    used = pl.program_id(0) < nu_ref[0]

    @pl.when(used)
    def _():
        o_ref[...] = _dot(g_ref[...], w_ref[0, 0])

    @pl.when(jnp.logical_not(used))
    def _():
        o_ref[...] = jnp.zeros_like(o_ref)


def _moe_experts(xs, block_expert, n_used, w_gate, w_up, w_down, layer, bm):
    p, d = xs.shape
    f = w_gate.shape[3]
    nblk = p // bm

    def row_idx(i, nu):
        return jnp.maximum(jnp.minimum(i, nu[0] - 1), 0)

    bn = _tile(f, COL_TILE_FUSED, LANES)
    nj = f // bn
    w_spec = pl.BlockSpec((1, 1, d, bn), lambda i, j, be, nu: (layer, be[i], 0, jnp.where(i < nu[0], j, nj - 1)))
    g = pl.pallas_call(
        _moe_up_kernel,
        grid_spec=pltpu.PrefetchScalarGridSpec(
            num_scalar_prefetch=2,
            grid=(nblk, nj),
            in_specs=[pl.BlockSpec((bm, d), lambda i, j, be, nu: (row_idx(i, nu), 0)), w_spec, w_spec],
            out_specs=pl.BlockSpec((bm, bn), lambda i, j, be, nu: (i, j)),
        ),
        out_shape=jax.ShapeDtypeStruct((p, f), BF16),
        compiler_params=_params("parallel", "arbitrary"),
        name="moe_up",
    )(block_expert, n_used, xs, w_gate, w_up)

    bn2 = _tile(d, MOE_DOWN_COLS, LANES)
    nj2 = d // bn2
    return pl.pallas_call(
        _moe_down_kernel,
        grid_spec=pltpu.PrefetchScalarGridSpec(
            num_scalar_prefetch=2,
            grid=(nblk, nj2),
            in_specs=[pl.BlockSpec((bm, f), lambda i, j, be, nu: (row_idx(i, nu), 0)),
                      pl.BlockSpec((1, 1, f, bn2),
                                   lambda i, j, be, nu: (layer, be[i], 0, jnp.where(i < nu[0], j, nj2 - 1)))],
            out_specs=pl.BlockSpec((bm, bn2), lambda i, j, be, nu: (i, j)),
        ),
        out_shape=jax.ShapeDtypeStruct((p, d), F32),
        compiler_params=_params("parallel", "arbitrary"),
        name="moe_down",
    )(block_expert, n_used, g, w_down)


def _combine_kernel(pos1_ref, pos2_ref, nxt1_ref, nxt2_ref, x_ref, meta_ref, gate_ref, y_ref, o_ref, buf_ref, sem,
                    *, rows, steps):
    i = pl.program_id(0)
    slot = i % 2

    def gather(p1_ref, p2_ref, s):
        def body(r, carry):
            _row_copy(y_ref, p1_ref[0, 0, r], buf_ref.at[s, 0], r, sem.at[s]).start()
            _row_copy(y_ref, p2_ref[0, 0, r], buf_ref.at[s, 1], r, sem.at[s]).start()
            return carry
        lax.fori_loop(0, rows, body, 0)

    @pl.when(i == 0)
    def _():
        gather(pos1_ref, pos2_ref, 0)

    @pl.when(i + 1 < steps)
    def _():
        gather(nxt1_ref, nxt2_ref, 1 - slot)

    def drain(r, carry):
        _row_copy(y_ref, 0, buf_ref.at[slot, 0], r, sem.at[slot]).wait()
        _row_copy(y_ref, 0, buf_ref.at[slot, 1], r, sem.at[slot]).wait()
        return carry

    lax.fori_loop(0, rows, drain, 0)
    w1 = meta_ref[:, 2:3]
    w2 = meta_ref[:, 3:4]
    o_ref[...] = x_ref[...] + gate_ref[0] * (w1 * buf_ref[slot, 0] + w2 * buf_ref[slot, 1])


def _combine(x2, y, pos1, pos2, meta, gate, seq):
    n, d = x2.shape
    rows = _tile(seq, GATHER_ROWS, SUBLANES)
    steps = n // rows
    pos_spec = pl.BlockSpec((1, 1, rows), lambda i: (i, 0, 0), memory_space=pltpu.SMEM)
    nxt_spec = pl.BlockSpec((1, 1, rows), lambda i: (jnp.minimum(i + 1, steps - 1), 0, 0), memory_space=pltpu.SMEM)
    p1 = pos1.reshape(steps, 1, rows)
    p2 = pos2.reshape(steps, 1, rows)
    return pl.pallas_call(
        functools.partial(_combine_kernel, rows=rows, steps=steps),
        grid=(steps,),
        in_specs=[pos_spec, pos_spec, nxt_spec, nxt_spec,
                  pl.BlockSpec((rows, d), lambda i: (i, 0)),
                  pl.BlockSpec((rows, LANES), lambda i: (i, 0)),
                  pl.BlockSpec((1, 1, d), lambda i: (i * rows // seq, 0, 0)),
                  pl.BlockSpec(memory_space=pl.ANY)],
        out_specs=pl.BlockSpec((rows, d), lambda i: (i, 0)),
        out_shape=jax.ShapeDtypeStruct((n, d), F32),
        scratch_shapes=[pltpu.VMEM((2, 2, rows, d), F32), pltpu.SemaphoreType.DMA((2,))],
        compiler_params=_params("arbitrary"),
        name="moe_combine",
    )(p1, p2, p1, p2, x2, meta, gate, y)


def _moe_layer(x2, h, gate, seq, w_router, b_router, w_gate, w_up, w_down, layer):
    n, d = h.shape
    experts = w_router.shape[1]
    meta, counts = _router(h, _pad_cols(w_router, LANES), _pad_cols(b_router.reshape(1, experts), LANES), experts)
    bm = _tile(2 * n, MOE_ROWS, SUBLANES)
    total_rows = 2 * n + experts * bm
    i1, i2 = meta[:, 0].astype(jnp.int32), meta[:, 1].astype(jnp.int32)
    r1, r2 = meta[:, 4].astype(jnp.int32), meta[:, 5].astype(jnp.int32)
    cnt = counts[0, :experts].astype(jnp.int32)
    padded = (cnt + bm - 1) // bm * bm
    pad_end = jnp.cumsum(padded)
    start = pad_end - padded
    valid_end = start + cnt
    expert_ids = jnp.arange(experts, dtype=jnp.int32)[None, :]
    start_of = lambda idx: jnp.sum(jnp.where(idx[:, None] == expert_ids, start[None, :], 0), axis=1)
    pos1 = start_of(i1) + r1
    pos2 = start_of(i2) + r2
    blk_start = jnp.arange(total_rows // bm, dtype=jnp.int32) * bm
    block_expert = jnp.minimum(jnp.sum(blk_start[:, None] >= pad_end[None, :], axis=1), experts - 1).astype(jnp.int32)
    n_used = (pad_end[-1:] // bm).astype(jnp.int32)

    xs = _dispatch(h, pos1, pos2, valid_end, pad_end, total_rows, bm)
    y = _moe_experts(xs, block_expert, n_used, w_gate, w_up, w_down, layer, bm)
    return _combine(x2, y, pos1, pos2, meta, gate, seq)


def kernel(x, c, cond_w, cond_b, ada_w, ada_b, mix_norm_g, ffn_norm_g, final_norm_g, conv_w_in, conv_b_in, conv_w_dw, conv_b_dw, conv_ln_g, conv_ln_b, conv_w_out, conv_b_out, mlstm_w_in, mlstm_b_gates, mlstm_norm_g, mlstm_w_out, ffn_w_gate, ffn_w_up, ffn_w_down, moe_w_router, moe_b_router, moe_w_gate, moe_w_up, moe_w_down):
    batch, seq, d = x.shape
    n = batch * seq
    depth = ada_w.shape[0]
    heads = mlstm_b_gates.shape[1] // 2
    dv = mlstm_norm_g.shape[1] // heads
    dk = (mlstm_w_in.shape[2] - 2 * heads * dv - 2 * heads) // (2 * heads)
    experts = moe_w_router.shape[2]
    assert dk % LANES == 0 and dv % LANES == 0 and 2 * heads <= LANES and experts <= LANES

    c_pad = jnp.pad(c, ((0, SUBLANES - batch % SUBLANES if batch % SUBLANES else 0), (0, 0)))
    e = _cond_embed(c_pad, cond_w, cond_b)
    mods = _cond_mod(e, ada_w, ada_b)[:, :batch].reshape(depth, batch, 6, 1, d)

    xf = x.reshape(n, d)
    conv_bias_in = conv_b_in.reshape(-1, 1, 2 * d)
    conv_bias_out = conv_b_out.reshape(-1, 1, d)
    ffn_down_bf16 = ffn_w_down.astype(BF16)
    moe_gate_bf16, moe_up_bf16, moe_down_bf16 = (w.astype(BF16) for w in (moe_w_gate, moe_w_up, moe_w_down))
    n_main = 2 * heads * (dk + dv)
    mlstm_w_in_t = jnp.swapaxes(mlstm_w_in, 1, 2)
    for i in range(depth):
        j = i // 2
        shift1, scale1, gate1, shift2, scale2, gate2 = [mods[i, :, t] for t in range(6)]

        h = _norm(xf, mix_norm_g[i], seq, scale1, shift1)
        if i % 2 == 0:
            u = _mm_gated(h, conv_w_in, conv_w_in, j, d, act="glu", out_dtype=F32, off1=0, off2=d, bias=conv_bias_in)
            v = _conv_ln_silu(u.reshape(batch, seq, d), conv_w_dw[j], conv_b_dw[j], conv_ln_g[j], conv_ln_b[j])
            xf = _mm_ws(v.reshape(n, d), conv_w_out, j, d, out_dtype=F32, bias=conv_bias_out,
                        res=xf, gate=gate1, seq=seq)
        else:
            w_gates_t = jnp.pad(mlstm_w_in_t[j, n_main:, :], ((0, LANES - 2 * heads), (0, 0)))
            b_gates = _pad_cols(mlstm_b_gates[j].reshape(1, 2 * heads), LANES)
            proj = _mm_ws(h, mlstm_w_in_t, j, n_main, out_dtype=BF16, transposed=True)
            gates = _mlstm_gates(h, w_gates_t, b_gates, heads)[:, :2 * heads]
            chunk = _tile(seq, MLSTM_CHUNK, LANES)
            gates_t = gates.reshape(batch, seq // chunk, chunk, 2 * heads).transpose(0, 3, 1, 2)
            y = _mlstm_cell(proj, gates_t[:, :, :, None, :], mlstm_norm_g[j], batch, seq, heads, dk, dv)
            xf = _mm_ws(y, mlstm_w_out, j, d, out_dtype=F32, res=xf, gate=gate1, seq=seq)

        h = _norm(xf, ffn_norm_g[i], seq, scale2, shift2, out_dtype=BF16 if i % 2 == 0 else F32)
        if i % 2 == 0:
            g = _mm_gated(h, ffn_w_gate, ffn_w_up, j, ffn_w_gate.shape[2], act="swiglu", out_dtype=BF16)
            xf = _mm_long(g, ffn_down_bf16, j, res=xf, gate=gate2, seq=seq)
        else:
            xf = _moe_layer(xf, h, gate2, seq, moe_w_router[j], moe_b_router[j],
                            moe_gate_bf16, moe_up_bf16, moe_down_bf16, j)

    out = _norm(xf, final_norm_g, seq, out_dtype=F32)
    return out.reshape(batch, seq, d)
```

```python
import functools

import jax
import jax.numpy as jnp
from jax import lax
from jax.experimental import pallas as pl
from jax.experimental.pallas import tpu as pltpu

EPS = 1e-6
F32 = jnp.float32
BF16 = jnp.bfloat16

LANES = 128
SUBLANES = 8
VMEM_LIMIT_BYTES = 56 * 1024 * 1024

ROW_TILE = 1024
COL_TILE_FUSED = 512
COL_TILE_WIDE = 1024
LONG_K_ROWS = 512
MOE_DOWN_COLS = 2048
LN_ROWS = 16
LN_UNROLL = 4
NORM_ROWS = 512
CONV_ROWS = 256
CONV_HALO = 32
CONV_GROUP = 64
MLSTM_CHUNK = 256
MOE_ROWS = 512
GATHER_ROWS = 256
ZERO_ROWS = 64


def _tile(dim, preferred, align):
    if dim <= preferred:
        return dim
    t = preferred - preferred % align
    while t > align and dim % t:
        t -= align
    assert dim % t == 0, (dim, preferred, align)
    return t


def _params(*sem):
    return pltpu.CompilerParams(dimension_semantics=sem, vmem_limit_bytes=VMEM_LIMIT_BYTES)


def _sigmoid(z):
    return 1.0 / (1.0 + jnp.exp(-z))


def _dot(a, b):
    return jnp.dot(a, b, preferred_element_type=F32)


def _dot_nt(a, b_t):
    return lax.dot_general(a, b_t, (((1,), (1,)), ((), ())), preferred_element_type=F32)


def _cond_embed_kernel(c_ref, w_ref, b_ref, o_ref):
    z = jnp.dot(c_ref[...], w_ref[...], preferred_element_type=F32,
                precision=lax.Precision.HIGHEST) + b_ref[...]
    o_ref[...] = z * _sigmoid(z)


def _cond_embed(c_pad, cond_w, cond_b):
    rows, r = c_pad.shape[0], cond_w.shape[1]
    return pl.pallas_call(
        _cond_embed_kernel,
        out_shape=jax.ShapeDtypeStruct((rows, r), F32),
        compiler_params=pltpu.CompilerParams(vmem_limit_bytes=VMEM_LIMIT_BYTES),
        name="cond_embed",
    )(c_pad, cond_w, cond_b.reshape(1, r))


def _cond_mod_kernel(e_ref, w_ref, b_ref, o_ref):
    o_ref[0] = jnp.dot(e_ref[...], w_ref[0], preferred_element_type=F32,
                       precision=lax.Precision.HIGHEST) + b_ref[0]


def _cond_mod(e, ada_w, ada_b):
    depth, r, width = ada_w.shape
    rows = e.shape[0]
    bn = _tile(width, 2048, LANES)
    return pl.pallas_call(
        _cond_mod_kernel,
        grid=(depth, width // bn),
        in_specs=[
            pl.BlockSpec((rows, r), lambda i, j: (0, 0)),
            pl.BlockSpec((1, r, bn), lambda i, j: (i, 0, j)),
            pl.BlockSpec((1, 1, bn), lambda i, j: (i, 0, j)),
        ],
        out_specs=pl.BlockSpec((1, rows, bn), lambda i, j: (i, 0, j)),
        out_shape=jax.ShapeDtypeStruct((depth, rows, width), F32),
        compiler_params=_params("parallel", "parallel"),
        name="cond_mod",
    )(e, ada_w, ada_b.reshape(depth, 1, width))


def _norm_kernel(x_ref, g_ref, *rest, modulate):
    if modulate:
        sc_ref, sh_ref, o_ref = rest
    else:
        (o_ref,) = rest
    x = x_ref[...]
    y = x * lax.rsqrt(jnp.mean(x * x, axis=-1, keepdims=True) + EPS) * g_ref[...]
    if modulate:
        y = y * (1.0 + sc_ref[0]) + sh_ref[0]
    o_ref[...] = y.astype(o_ref.dtype)


def _norm(x2, g, seq, scale=None, shift=None, out_dtype=BF16):
    n, d = x2.shape
    tm = _tile(seq, NORM_ROWS, SUBLANES)
    modulate = scale is not None
    in_specs = [pl.BlockSpec((tm, d), lambda i: (i, 0)),
                pl.BlockSpec((1, d), lambda i: (0, 0))]
    args = [x2, g.reshape(1, d)]
    if modulate:
        per_batch = pl.BlockSpec((1, 1, d), lambda i: (i * tm // seq, 0, 0))
        in_specs += [per_batch, per_batch]
        args += [scale, shift]
    return pl.pallas_call(
        functools.partial(_norm_kernel, modulate=modulate),
        grid=(n // tm,),
        in_specs=in_specs,
        out_specs=pl.BlockSpec((tm, d), lambda i: (i, 0)),
        out_shape=jax.ShapeDtypeStruct((n, d), out_dtype),
        compiler_params=_params("parallel"),
        name="norm_mod" if modulate else "rms_norm",
    )(*args)


def _stage_weights(w_hbm, layer, col_off, stage_ref, wb_ref, sem, *, bn, nj, transposed=False):
    j = pl.program_id(0)
    first_row_block = pl.program_id(1) == 0

    def fetch(jj):
        cols = pl.ds(pl.multiple_of((jj + col_off) * bn, bn), bn)
        src = w_hbm.at[layer, cols, :] if transposed else w_hbm.at[layer, :, cols]
        return pltpu.make_async_copy(src, stage_ref, sem)

    @pl.when(jnp.logical_and(first_row_block, j == 0))
    def _():
        fetch(0).start()

    @pl.when(first_row_block)
    def _():
        fetch(j).wait()
        wb_ref[...] = stage_ref[...].astype(BF16)

        @pl.when(j + 1 < nj)
        def _():
            fetch(j + 1).start()


def _mm_gated_kernel(a_ref, w1_hbm, w2_hbm, *rest, act, has_bias, layer, o1, o2, bn, nj):
    rest = list(rest)
    sem = rest.pop()
    w2b_ref = rest.pop()
    w1b_ref = rest.pop()
    stage2_ref = rest.pop()
    stage1_ref = rest.pop()
    o_ref = rest.pop()
    _stage_weights(w1_hbm, layer, o1, stage1_ref, w1b_ref, sem.at[0], bn=bn, nj=nj)
    _stage_weights(w2_hbm, layer, o2, stage2_ref, w2b_ref, sem.at[1], bn=bn, nj=nj)

    a = a_ref[...]
    z1 = _dot(a, w1b_ref[...])
    z2 = _dot(a, w2b_ref[...])
    if has_bias:
        z1 = z1 + rest[0][0]
        z2 = z2 + rest[1][0]
    if act == "glu":
        y = z1 * _sigmoid(z2)
    else:
        y = (z1 * _sigmoid(z1)) * z2
    o_ref[...] = y.astype(o_ref.dtype)


def _mm_gated(a, w1, w2, layer, n_out, *, act, out_dtype, off1=0, off2=0, bias=None):
    m, k = a.shape
    bm = _tile(m, ROW_TILE, SUBLANES)
    bn = _tile(n_out, COL_TILE_FUSED, LANES)
    assert off1 % bn == 0 and off2 % bn == 0
    o1, o2 = off1 // bn, off2 // bn
    nj = n_out // bn
    in_specs = [
        pl.BlockSpec((bm, k), lambda j, i: (i, 0)),
        pl.BlockSpec(memory_space=pl.ANY),
        pl.BlockSpec(memory_space=pl.ANY),
    ]
    args = [a, w1, w2]
    if bias is not None:
        in_specs += [pl.BlockSpec((1, 1, bn), lambda j, i: (layer, 0, j + o1)),
                     pl.BlockSpec((1, 1, bn), lambda j, i: (layer, 0, j + o2))]
        args += [bias, bias]
    return pl.pallas_call(
        functools.partial(_mm_gated_kernel, act=act, has_bias=bias is not None,
                          layer=layer, o1=o1, o2=o2, bn=bn, nj=nj),
        grid=(nj, m // bm),
        in_specs=in_specs,
        out_specs=pl.BlockSpec((bm, bn), lambda j, i: (i, j)),
        out_shape=jax.ShapeDtypeStruct((m, n_out), out_dtype),
        scratch_shapes=[pltpu.VMEM((k, bn), F32), pltpu.VMEM((k, bn), F32),
                        pltpu.VMEM((k, bn), BF16), pltpu.VMEM((k, bn), BF16), pltpu.SemaphoreType.DMA((2,))],
        compiler_params=_params("arbitrary", "arbitrary"),
        name="mm_" + act,
    )(*args)


def _mm_ws_kernel(a_ref, w_hbm, *rest, has_bias, has_res, layer, bn, nj, transposed):
    rest = list(rest)
    sem = rest.pop()
    wb_ref = rest.pop()
    stage_ref = rest.pop()
    o_ref = rest.pop()
    _stage_weights(w_hbm, layer, 0, stage_ref, wb_ref, sem, bn=bn, nj=nj, transposed=transposed)

    acc = _dot_nt(a_ref[...], wb_ref[...]) if transposed else _dot(a_ref[...], wb_ref[...])
    idx = 0
    if has_bias:
        acc = acc + rest[idx][0]
        idx += 1
    if has_res:
        acc = rest[idx][...] + rest[idx + 1][0] * acc
    o_ref[...] = acc.astype(o_ref.dtype)


def _mm_ws(a, w, layer, n_out, *, out_dtype, bias=None, res=None, gate=None, seq=None, transposed=False):
    m, k = a.shape
    bm = _tile(m if seq is None else seq, ROW_TILE, SUBLANES)
    bn = _tile(n_out, COL_TILE_FUSED if res is not None else COL_TILE_WIDE, LANES)
    nj = n_out // bn
    w_block = (bn, k) if transposed else (k, bn)
    in_specs = [pl.BlockSpec((bm, k), lambda j, i: (i, 0)),
                pl.BlockSpec(memory_space=pl.ANY)]
    args = [a, w]
    if bias is not None:
        in_specs.append(pl.BlockSpec((1, 1, bn), lambda j, i: (layer, 0, j)))
        args.append(bias)
    if res is not None:
        in_specs += [pl.BlockSpec((bm, bn), lambda j, i: (i, j)),
                     pl.BlockSpec((1, 1, bn), lambda j, i: (i * bm // seq, 0, j))]
        args += [res, gate]
    return pl.pallas_call(
        functools.partial(_mm_ws_kernel, has_bias=bias is not None, has_res=res is not None,
                          layer=layer, bn=bn, nj=nj, transposed=transposed),
        grid=(nj, m // bm),
        in_specs=in_specs,
        out_specs=pl.BlockSpec((bm, bn), lambda j, i: (i, j)),
        out_shape=jax.ShapeDtypeStruct((m, n_out), out_dtype),
        scratch_shapes=[pltpu.VMEM(w_block, F32), pltpu.VMEM(w_block, BF16), pltpu.SemaphoreType.DMA],
        compiler_params=_params("arbitrary", "arbitrary"),
        name="mm_res" if res is not None else "mm",
    )(*args)


def _mm_long_kernel(a_ref, w_ref, res_ref, gate_ref, o_ref):
    o_ref[...] = res_ref[...] + gate_ref[0] * _dot(a_ref[...], w_ref[0])


def _mm_long(a, w, layer, *, res, gate, seq):
    m, k = a.shape
    n_out = w.shape[2]
    bm = _tile(seq, LONG_K_ROWS, SUBLANES)
    bn = _tile(n_out, COL_TILE_FUSED, LANES)
    return pl.pallas_call(
        _mm_long_kernel,
        grid=(m // bm, n_out // bn),
        in_specs=[pl.BlockSpec((bm, k), lambda i, j: (i, 0)),
                  pl.BlockSpec((1, k, bn), lambda i, j: (layer, 0, j)),
                  pl.BlockSpec((bm, bn), lambda i, j: (i, j)),
                  pl.BlockSpec((1, 1, bn), lambda i, j: (i * bm // seq, 0, j))],
        out_specs=pl.BlockSpec((bm, bn), lambda i, j: (i, j)),
        out_shape=jax.ShapeDtypeStruct((m, n_out), F32),
        compiler_params=_params("parallel", "arbitrary"),
        name="mm_long",
    )(a, w, res, gate)


def _conv_kernel(u_ref, halo_ref, w_ref, bdw_ref, g_ref, b_ref, o_ref, buf_ref, y_ref, *, ts, kw, lc):
    d = y_ref.shape[1]
    s = pl.program_id(1)

    @pl.when(s == 0)
    def _():
        buf_ref[0:CONV_HALO, :] = jnp.zeros((CONV_HALO, d), F32)

    @pl.when(s > 0)
    def _():
        buf_ref[0:CONV_HALO, :] = halo_ref[0]

    buf_ref[CONV_HALO:, :] = u_ref[0]
    base = CONV_HALO - (kw - 1)

    def lane_chunk(c, carry):
        l0 = pl.multiple_of(c * lc, lc)
        lanes = pl.ds(l0, lc)
        bias = bdw_ref[:, lanes]
        win_rows = CONV_GROUP + CONV_HALO
        for g0 in range(0, ts, CONV_GROUP):
            win = buf_ref[pl.ds(g0, win_rows), lanes]
            acc = jnp.broadcast_to(bias, (CONV_GROUP, lc))
            for r in range(SUBLANES):
                shifted = win if r == 0 else pltpu.roll(win, win_rows - r, axis=0)
                for a in range(CONV_HALO // SUBLANES + 1):
                    j = SUBLANES * a + r
                    if base <= j < base + kw:
                        acc = acc + w_ref[j - base:j - base + 1, lanes] * shifted[SUBLANES * a:SUBLANES * a + CONV_GROUP]
            y_ref[pl.ds(g0, CONV_GROUP), lanes] = acc
        return carry

    lax.fori_loop(0, d // lc, lane_chunk, 0)

    def layer_norm_rows(t, carry):
        rows = pl.ds(pl.multiple_of(t * LN_ROWS, LN_ROWS), LN_ROWS)
        y = y_ref[rows, :]
        mu = jnp.mean(y, axis=-1, keepdims=True)
        yc = y - mu
        var = jnp.mean(yc * yc, axis=-1, keepdims=True)
        z = yc * lax.rsqrt(var + EPS) * g_ref[...] + b_ref[...]
        o_ref[0, rows, :] = (z * _sigmoid(z)).astype(o_ref.dtype)
        return carry

    lax.fori_loop(0, ts // LN_ROWS, layer_norm_rows, 0, unroll=LN_UNROLL)


def _conv_ln_silu(u3, w_dw, b_dw, ln_g, ln_b):
    b, s, d = u3.shape
    kw = w_dw.shape[0]
    assert kw - 1 <= CONV_HALO
    ts = _tile(s, CONV_ROWS, CONV_GROUP)
    assert ts % LN_ROWS == 0
    halo_blocks = ts // CONV_HALO
    vec = pl.BlockSpec((1, d), lambda bi, si: (0, 0))
    return pl.pallas_call(
        functools.partial(_conv_kernel, ts=ts, kw=kw, lc=LANES),
        grid=(b, s // ts),
        in_specs=[
            pl.BlockSpec((1, ts, d), lambda bi, si: (bi, si, 0)),
            pl.BlockSpec((1, CONV_HALO, d), lambda bi, si: (bi, jnp.maximum(si * halo_blocks - 1, 0), 0)),
            pl.BlockSpec((kw, d), lambda bi, si: (0, 0)),
            vec, vec, vec,
        ],
        out_specs=pl.BlockSpec((1, ts, d), lambda bi, si: (bi, si, 0)),
        out_shape=jax.ShapeDtypeStruct((b, s, d), BF16),
        scratch_shapes=[pltpu.VMEM((ts + CONV_HALO, d), F32), pltpu.VMEM((ts, d), F32)],
        compiler_params=_params("parallel", "arbitrary"),
        name="dwconv_ln_silu",
    )(u3, u3, w_dw, b_dw.reshape(1, d), ln_g.reshape(1, d), ln_b.reshape(1, d))


def _gates_kernel(a_ref, w_ref, b_ref, o_ref, *, heads):
    z = _dot_nt(a_ref[...], w_ref[...].astype(BF16)) + b_ref[...]
    lane = lax.broadcasted_iota(jnp.int32, z.shape, 1)
    log_sig = -(jnp.maximum(-z, 0.0) + jnp.log1p(jnp.exp(-jnp.abs(z))))
    o_ref[...] = jnp.where((lane >= heads) & (lane < 2 * heads), log_sig, z)


def _mlstm_gates(h, w_pad_t, b_pad, heads):
    m, k = h.shape
    bm = _tile(m, ROW_TILE, SUBLANES)
    return pl.pallas_call(
        functools.partial(_gates_kernel, heads=heads),
        grid=(m // bm,),
        in_specs=[pl.BlockSpec((bm, k), lambda i: (i, 0)),
                  pl.BlockSpec((LANES, k), lambda i: (0, 0)),
                  pl.BlockSpec((1, LANES), lambda i: (0, 0))],
        out_specs=pl.BlockSpec((bm, LANES), lambda i: (i, 0)),
        out_shape=jax.ShapeDtypeStruct((m, LANES), F32),
        compiler_params=_params("parallel"),
        name="mlstm_gates",
    )(h, w_pad_t, b_pad)


def _mlstm_kernel(q_ref, k_ref, v_ref, o_ref, igr_ref, lfr_ref, ng_ref,
                  out_ref, c_ref, n_ref, m_ref, *, chunk, dk):
    @pl.when(pl.program_id(2) == 0)
    def _():
        c_ref[...] = jnp.zeros_like(c_ref)
        n_ref[...] = jnp.zeros_like(n_ref)
        m_ref[...] = jnp.zeros_like(m_ref)

    q = q_ref[...]
    k = k_ref[...]
    v = v_ref[...]
    ig_r, lf_r = igr_ref[...], lfr_ref[...]
    row = lax.broadcasted_iota(jnp.int32, (chunk, chunk), 0)
    col = lax.broadcasted_iota(jnp.int32, (chunk, chunk), 1)
    causal = col <= row
    diag = row == col
    ig_c = jnp.sum(jnp.where(diag, ig_r, 0.0), axis=1, keepdims=True)
    lf_c = jnp.sum(jnp.where(diag, lf_r, 0.0), axis=1, keepdims=True)
    b_c = jnp.sum(jnp.where(causal, lf_r, 0.0), axis=1, keepdims=True)
    b_r = jnp.sum(jnp.where(row <= col, lf_c, 0.0), axis=0, keepdims=True)
    b_end = jnp.sum(lf_r, axis=1, keepdims=True)
    m_prev = m_ref[...]
    a_c = b_c + m_prev
    dmat = jnp.where(causal, b_c - b_r + ig_r, -jnp.inf)
    m_s = jnp.maximum(a_c, jnp.max(dmat, axis=1, keepdims=True))
    w_inter = jnp.exp(a_c - m_s)
    scale = dk ** -0.5
    p = jnp.exp(dmat - m_s) * scale
    s = lax.dot_general(q, k, (((1,), (1,)), ((), ())), preferred_element_type=F32) * p
    c_state = c_ref[...]
    n_state = n_ref[...]
    num = _dot(s.astype(BF16), v) + w_inter * _dot(q, c_state.astype(BF16))
    den = (jnp.sum(s, axis=1, keepdims=True)
           + w_inter * jnp.sum(q.astype(F32) * n_state, axis=1, keepdims=True))
    h = num / jnp.maximum(jnp.abs(den), jnp.exp(-m_s))
    hn = h * lax.rsqrt(jnp.mean(h * h, axis=1, keepdims=True) + EPS) * ng_ref[...]
    out_ref[...] = (hn * _sigmoid(o_ref[...].astype(F32))).astype(out_ref.dtype)

    g_c = b_end - b_c + ig_c
    m_new = jnp.maximum(b_end + m_prev, jnp.max(g_c, axis=0, keepdims=True))
    decay = jnp.exp(b_end + m_prev - m_new)
    wk = k.astype(F32) * (jnp.exp(g_c - m_new) * scale)
    c_ref[...] = decay * c_state + lax.dot_general(
        wk.astype(BF16), v, (((0,), (0,)), ((), ())), preferred_element_type=F32)
    n_ref[...] = decay * n_state + jnp.sum(wk, axis=0, keepdims=True)
    m_ref[...] = m_new


def _mlstm_cell(proj, gates_rows, norm_g, batch, seq, heads, dk, dv):
    n = proj.shape[0]
    chunk = _tile(seq, MLSTM_CHUNK, LANES)
    nc = seq // chunk
    v_off = 2 * heads * dk // dv
    row_blk = lambda b, h, c: b * nc + c
    g_row = lambda off: pl.BlockSpec((None, None, None, 1, chunk), lambda b, h, c: (b, off + h, c, 0, 0))
    return pl.pallas_call(
        functools.partial(_mlstm_kernel, chunk=chunk, dk=dk),
        grid=(batch, heads, nc),
        in_specs=[
            pl.BlockSpec((chunk, dk), lambda b, h, c: (row_blk(b, h, c), h)),
            pl.BlockSpec((chunk, dk), lambda b, h, c: (row_blk(b, h, c), heads + h)),
            pl.BlockSpec((chunk, dv), lambda b, h, c: (row_blk(b, h, c), v_off + h)),
            pl.BlockSpec((chunk, dv), lambda b, h, c: (row_blk(b, h, c), v_off + heads + h)),
            g_row(0), g_row(heads),
            pl.BlockSpec((1, dv), lambda b, h, c: (0, h)),
        ],
        out_specs=pl.BlockSpec((chunk, dv), lambda b, h, c: (row_blk(b, h, c), h)),
        out_shape=jax.ShapeDtypeStruct((n, heads * dv), BF16),
        scratch_shapes=[pltpu.VMEM((dk, dv), F32), pltpu.VMEM((1, dk), F32), pltpu.VMEM((1, 1), F32)],
        compiler_params=_params("parallel", "parallel", "arbitrary"),
        name="mlstm_cell",
    )(proj, proj, proj, proj, gates_rows, gates_rows, norm_g.reshape(1, heads * dv))


def _pad_cols(w, width):
    return jnp.pad(w, ((0, 0), (0, width - w.shape[1])))


def _router_kernel(a_ref, w_ref, b_ref, meta_ref, cnt_ref, carry_ref, *, experts):
    @pl.when(pl.program_id(0) == 0)
    def _():
        carry_ref[...] = jnp.zeros_like(carry_ref)

    a = a_ref[...]
    w = w_ref[...]
    a_hi = a.astype(BF16)
    a_lo = (a - a_hi.astype(F32)).astype(BF16)
    w_hi = w.astype(BF16)
    w_lo = (w - w_hi.astype(F32)).astype(BF16)
    logits = _dot(a_hi, w_hi) + (_dot(a_hi, w_lo) + _dot(a_lo, w_hi)) + b_ref[...]
    rows = logits.shape[0]
    lane = lax.broadcasted_iota(jnp.int32, logits.shape, 1)
    neg = -jnp.inf
    logits = jnp.where(lane < experts, logits, neg)
    v1 = jnp.max(logits, axis=1, keepdims=True)
    i1 = jnp.min(jnp.where(logits == v1, lane, LANES), axis=1, keepdims=True)
    rest = jnp.where(lane == i1, neg, logits)
    v2 = jnp.max(rest, axis=1, keepdims=True)
    i2 = jnp.min(jnp.where(rest == v2, lane, LANES), axis=1, keepdims=True)
    e2 = jnp.exp(v2 - v1)
    w1 = 1.0 / (1.0 + e2)
    w2 = e2 / (1.0 + e2)

    oh1 = jnp.where(lane == i1, 1.0, 0.0)
    oh2 = jnp.where(lane == i2, 1.0, 0.0)
    oh = oh1 + oh2
    tr = lax.broadcasted_iota(jnp.int32, (rows, rows), 0)
    tc = lax.broadcasted_iota(jnp.int32, (rows, rows), 1)
    tri = jnp.where(tc < tr, 1.0, 0.0).astype(BF16)
    rank = _dot(tri, oh.astype(BF16)) + carry_ref[...]
    r1 = jnp.sum(oh1 * rank, axis=1, keepdims=True)
    r2 = jnp.sum(oh2 * rank, axis=1, keepdims=True)
    carry_ref[...] += jnp.sum(oh, axis=0, keepdims=True)
    cnt_ref[...] = carry_ref[...]

    fields = (i1.astype(F32), i2.astype(F32), w1, w2, r1, r2)
    meta = jnp.zeros(logits.shape, F32)
    for idx, val in enumerate(fields):
        meta = jnp.where(lane == idx, val, meta)
    meta_ref[...] = meta


def _router(h, w_pad, b_pad, experts):
    m, k = h.shape
    bm = _tile(m, ROW_TILE, SUBLANES)
    return pl.pallas_call(
        functools.partial(_router_kernel, experts=experts),
        grid=(m // bm,),
        in_specs=[pl.BlockSpec((bm, k), lambda i: (i, 0)),
                  pl.BlockSpec((k, LANES), lambda i: (0, 0)),
                  pl.BlockSpec((1, LANES), lambda i: (0, 0))],
        out_specs=[pl.BlockSpec((bm, LANES), lambda i: (i, 0)),
                   pl.BlockSpec((1, LANES), lambda i: (0, 0))],
        out_shape=[jax.ShapeDtypeStruct((m, LANES), F32), jax.ShapeDtypeStruct((1, LANES), F32)],
        scratch_shapes=[pltpu.VMEM((1, LANES), F32)],
        compiler_params=_params("arbitrary"),
        name="moe_router",
    )(h, w_pad, b_pad)


def _row_copy(src_ref, src_row, dst_ref, dst_row, sem):
    return pltpu.make_async_copy(src_ref.at[pl.ds(src_row, 1)], dst_ref.at[pl.ds(dst_row, 1)], sem)


def _dispatch_kernel(valid_end_ref, pad_end_ref, pos1_ref, pos2_ref, h_ref, xs_ref, zero_ref, sem, zero_sem,
                     *, rows, experts):
    @pl.when(pl.program_id(0) == 0)
    def _():
        zero_ref[...] = jnp.zeros_like(zero_ref)
        for e in range(experts):
            lo, hi = valid_end_ref[e], pad_end_ref[e]

            def fill(p, carry):
                _row_copy(zero_ref, 0, xs_ref, p, zero_sem).start()
                return carry

            def fill_wait(p, carry):
                _row_copy(zero_ref, 0, xs_ref, p, zero_sem).wait()
                return carry

            lax.fori_loop(lo, hi, fill, 0)
            lax.fori_loop(lo, hi, fill_wait, 0)

        zr = zero_ref.shape[0]
        tail = pltpu.make_async_copy
        first = pad_end_ref[experts - 1] // zr
        last = xs_ref.shape[0] // zr

        def tail_fill(b, carry):
            tail(zero_ref, xs_ref.at[pl.ds(pl.multiple_of(b * zr, zr), zr)], zero_sem).start()
            return carry

        def tail_wait(b, carry):
            tail(zero_ref, xs_ref.at[pl.ds(pl.multiple_of(b * zr, zr), zr)], zero_sem).wait()
            return carry

        lax.fori_loop(first, last, tail_fill, 0)
        lax.fori_loop(first, last, tail_wait, 0)

    def issue(r, carry):
        _row_copy(h_ref, r, xs_ref, pos1_ref[0, 0, r], sem).start()
        _row_copy(h_ref, r, xs_ref, pos2_ref[0, 0, r], sem).start()
        return carry

    def drain(r, carry):
        _row_copy(h_ref, r, xs_ref, pos1_ref[0, 0, r], sem).wait()
        _row_copy(h_ref, r, xs_ref, pos2_ref[0, 0, r], sem).wait()
        return carry

    lax.fori_loop(0, rows, issue, 0)
    lax.fori_loop(0, rows, drain, 0)


def _dispatch(h, pos1, pos2, valid_end, pad_end, total_rows, bm):
    n, d = h.shape
    rows = _tile(n, GATHER_ROWS, SUBLANES)
    steps = n // rows
    experts = valid_end.shape[0]
    zero_rows = _tile(bm, ZERO_ROWS, SUBLANES)
    pos_spec = pl.BlockSpec((1, 1, rows), lambda i, ve, pe: (i, 0, 0), memory_space=pltpu.SMEM)
    return pl.pallas_call(
        functools.partial(_dispatch_kernel, rows=rows, experts=experts),
        grid_spec=pltpu.PrefetchScalarGridSpec(
            num_scalar_prefetch=2,
            grid=(steps,),
            in_specs=[pos_spec, pos_spec, pl.BlockSpec((rows, d), lambda i, ve, pe: (i, 0))],
            out_specs=pl.BlockSpec(memory_space=pl.ANY),
            scratch_shapes=[pltpu.VMEM((zero_rows, d), F32), pltpu.SemaphoreType.DMA, pltpu.SemaphoreType.DMA],
        ),
        out_shape=jax.ShapeDtypeStruct((total_rows, d), F32),
        compiler_params=_params("arbitrary"),
        name="moe_dispatch",
    )(valid_end, pad_end, pos1.reshape(steps, 1, rows), pos2.reshape(steps, 1, rows), h)


def _moe_up_kernel(be_ref, nu_ref, a_ref, wg_ref, wu_ref, o_ref):
    used = pl.program_id(0) < nu_ref[0]

    @pl.when(used)
    def _():
        a = a_ref[...].astype(BF16)
        z1 = _dot(a, wg_ref[0, 0])
        z2 = _dot(a, wu_ref[0, 0])
        o_ref[...] = ((z1 * _sigmoid(z1)) * z2).astype(o_ref.dtype)

    @pl.when(jnp.logical_not(used))
    def _():
        o_ref[...] = jnp.zeros_like(o_ref)


def _moe_down_kernel(be_ref, nu_ref, g_ref, w_ref, o_ref):
    used = pl.program_id(0) < nu_ref[0]

    @pl.when(used)
    def _():
        o_ref[...] = _dot(g_ref[...], w_ref[0, 0])

    @pl.when(jnp.logical_not(used))
    def _():
        o_ref[...] = jnp.zeros_like(o_ref)


def _moe_experts(xs, block_expert, n_used, w_gate, w_up, w_down, layer, bm):
    p, d = xs.shape
    f = w_gate.shape[3]
    nblk = p // bm

    def row_idx(i, nu):
        return jnp.maximum(jnp.minimum(i, nu[0] - 1), 0)

    bn = _tile(f, COL_TILE_FUSED, LANES)
    nj = f // bn
    w_spec = pl.BlockSpec((1, 1, d, bn), lambda i, j, be, nu: (layer, be[i], 0, jnp.where(i < nu[0], j, nj - 1)))
    g = pl.pallas_call(
        _moe_up_kernel,
        grid_spec=pltpu.PrefetchScalarGridSpec(
            num_scalar_prefetch=2,
            grid=(nblk, nj),
            in_specs=[pl.BlockSpec((bm, d), lambda i, j, be, nu: (row_idx(i, nu), 0)), w_spec, w_spec],
            out_specs=pl.BlockSpec((bm, bn), lambda i, j, be, nu: (i, j)),
        ),
        out_shape=jax.ShapeDtypeStruct((p, f), BF16),
        compiler_params=_params("parallel", "arbitrary"),
        name="moe_up",
    )(block_expert, n_used, xs, w_gate, w_up)

    bn2 = _tile(d, MOE_DOWN_COLS, LANES)
    nj2 = d // bn2
    return pl.pallas_call(
        _moe_down_kernel,
        grid_spec=pltpu.PrefetchScalarGridSpec(
            num_scalar_prefetch=2,
            grid=(nblk, nj2),
            in_specs=[pl.BlockSpec((bm, f), lambda i, j, be, nu: (row_idx(i, nu), 0)),
                      pl.BlockSpec((1, 1, f, bn2),
                                   lambda i, j, be, nu: (layer, be[i], 0, jnp.where(i < nu[0], j, nj2 - 1)))],
            out_specs=pl.BlockSpec((bm, bn2), lambda i, j, be, nu: (i, j)),
        ),
        out_shape=jax.ShapeDtypeStruct((p, d), F32),
        compiler_params=_params("parallel", "arbitrary"),
        name="moe_down",
    )(block_expert, n_used, g, w_down)


def _combine_kernel(pos1_ref, pos2_ref, nxt1_ref, nxt2_ref, x_ref, meta_ref, gate_ref, y_ref, o_ref, buf_ref, sem,
                    *, rows, steps):
    i = pl.program_id(0)
    slot = i % 2

    def gather(p1_ref, p2_ref, s):
        def body(r, carry):
            _row_copy(y_ref, p1_ref[0, 0, r], buf_ref.at[s, 0], r, sem.at[s]).start()
            _row_copy(y_ref, p2_ref[0, 0, r], buf_ref.at[s, 1], r, sem.at[s]).start()
            return carry
        lax.fori_loop(0, rows, body, 0)

    @pl.when(i == 0)
    def _():
        gather(pos1_ref, pos2_ref, 0)

    @pl.when(i + 1 < steps)
    def _():
        gather(nxt1_ref, nxt2_ref, 1 - slot)

    def drain(r, carry):
        _row_copy(y_ref, 0, buf_ref.at[slot, 0], r, sem.at[slot]).wait()
        _row_copy(y_ref, 0, buf_ref.at[slot, 1], r, sem.at[slot]).wait()
        return carry

    lax.fori_loop(0, rows, drain, 0)
    w1 = meta_ref[:, 2:3]
    w2 = meta_ref[:, 3:4]
    o_ref[...] = x_ref[...] + gate_ref[0] * (w1 * buf_ref[slot, 0] + w2 * buf_ref[slot, 1])


def _combine(x2, y, pos1, pos2, meta, gate, seq):
    n, d = x2.shape
    rows = _tile(seq, GATHER_ROWS, SUBLANES)
    steps = n // rows
    pos_spec = pl.BlockSpec((1, 1, rows), lambda i: (i, 0, 0), memory_space=pltpu.SMEM)
    nxt_spec = pl.BlockSpec((1, 1, rows), lambda i: (jnp.minimum(i + 1, steps - 1), 0, 0), memory_space=pltpu.SMEM)
    p1 = pos1.reshape(steps, 1, rows)
    p2 = pos2.reshape(steps, 1, rows)
    return pl.pallas_call(
        functools.partial(_combine_kernel, rows=rows, steps=steps),
        grid=(steps,),
        in_specs=[pos_spec, pos_spec, nxt_spec, nxt_spec,
                  pl.BlockSpec((rows, d), lambda i: (i, 0)),
                  pl.BlockSpec((rows, LANES), lambda i: (i, 0)),
                  pl.BlockSpec((1, 1, d), lambda i: (i * rows // seq, 0, 0)),
                  pl.BlockSpec(memory_space=pl.ANY)],
        out_specs=pl.BlockSpec((rows, d), lambda i: (i, 0)),
        out_shape=jax.ShapeDtypeStruct((n, d), F32),
        scratch_shapes=[pltpu.VMEM((2, 2, rows, d), F32), pltpu.SemaphoreType.DMA((2,))],
        compiler_params=_params("arbitrary"),
        name="moe_combine",
    )(p1, p2, p1, p2, x2, meta, gate, y)


def _moe_layer(x2, h, gate, seq, w_router, b_router, w_gate, w_up, w_down, layer):
    n, d = h.shape
    experts = w_router.shape[1]
    meta, counts = _router(h, _pad_cols(w_router, LANES), _pad_cols(b_router.reshape(1, experts), LANES), experts)
    bm = _tile(2 * n, MOE_ROWS, SUBLANES)
    total_rows = 2 * n + experts * bm
    i1, i2 = meta[:, 0].astype(jnp.int32), meta[:, 1].astype(jnp.int32)
    r1, r2 = meta[:, 4].astype(jnp.int32), meta[:, 5].astype(jnp.int32)
    cnt = counts[0, :experts].astype(jnp.int32)
    padded = (cnt + bm - 1) // bm * bm
    pad_end = jnp.cumsum(padded)
    start = pad_end - padded
    valid_end = start + cnt
    expert_ids = jnp.arange(experts, dtype=jnp.int32)[None, :]
    start_of = lambda idx: jnp.sum(jnp.where(idx[:, None] == expert_ids, start[None, :], 0), axis=1)
    pos1 = start_of(i1) + r1
    pos2 = start_of(i2) + r2
    blk_start = jnp.arange(total_rows // bm, dtype=jnp.int32) * bm
    block_expert = jnp.minimum(jnp.sum(blk_start[:, None] >= pad_end[None, :], axis=1), experts - 1).astype(jnp.int32)
    n_used = (pad_end[-1:] // bm).astype(jnp.int32)

    xs = _dispatch(h, pos1, pos2, valid_end, pad_end, total_rows, bm)
    y = _moe_experts(xs, block_expert, n_used, w_gate, w_up, w_down, layer, bm)
    return _combine(x2, y, pos1, pos2, meta, gate, seq)


def kernel(x, c, cond_w, cond_b, ada_w, ada_b, mix_norm_g, ffn_norm_g, final_norm_g, conv_w_in, conv_b_in, conv_w_dw, conv_b_dw, conv_ln_g, conv_ln_b, conv_w_out, conv_b_out, mlstm_w_in, mlstm_b_gates, mlstm_norm_g, mlstm_w_out, ffn_w_gate, ffn_w_up, ffn_w_down, moe_w_router, moe_b_router, moe_w_gate, moe_w_up, moe_w_down):
    batch, seq, d = x.shape
    n = batch * seq
    depth = ada_w.shape[0]
    heads = mlstm_b_gates.shape[1] // 2
    dv = mlstm_norm_g.shape[1] // heads
    dk = (mlstm_w_in.shape[2] - 2 * heads * dv - 2 * heads) // (2 * heads)
    experts = moe_w_router.shape[2]
    assert dk % LANES == 0 and dv % LANES == 0 and 2 * heads <= LANES and experts <= LANES

    c_pad = jnp.pad(c, ((0, SUBLANES - batch % SUBLANES if batch % SUBLANES else 0), (0, 0)))
    e = _cond_embed(c_pad, cond_w, cond_b)
    mods = _cond_mod(e, ada_w, ada_b)[:, :batch].reshape(depth, batch, 6, 1, d)

    xf = x.reshape(n, d)
    conv_bias_in = conv_b_in.reshape(-1, 1, 2 * d)
    conv_bias_out = conv_b_out.reshape(-1, 1, d)
    ffn_down_bf16 = ffn_w_down.astype(BF16)
    moe_gate_bf16, moe_up_bf16, moe_down_bf16 = (w.astype(BF16) for w in (moe_w_gate, moe_w_up, moe_w_down))
    n_main = 2 * heads * (dk + dv)
    mlstm_w_in_t = jnp.swapaxes(mlstm_w_in, 1, 2)
    for i in range(depth):
        j = i // 2
        shift1, scale1, gate1, shift2, scale2, gate2 = [mods[i, :, t] for t in range(6)]

        h = _norm(xf, mix_norm_g[i], seq, scale1, shift1)
        if i % 2 == 0:
            u = _mm_gated(h, conv_w_in, conv_w_in, j, d, act="glu", out_dtype=F32, off1=0, off2=d, bias=conv_bias_in)
            v = _conv_ln_silu(u.reshape(batch, seq, d), conv_w_dw[j], conv_b_dw[j], conv_ln_g[j], conv_ln_b[j])
            xf = _mm_ws(v.reshape(n, d), conv_w_out, j, d, out_dtype=F32, bias=conv_bias_out,
                        res=xf, gate=gate1, seq=seq)
        else:
            w_gates_t = jnp.pad(mlstm_w_in_t[j, n_main:, :], ((0, LANES - 2 * heads), (0, 0)))
            b_gates = _pad_cols(mlstm_b_gates[j].reshape(1, 2 * heads), LANES)
            proj = _mm_ws(h, mlstm_w_in_t, j, n_main, out_dtype=BF16, transposed=True)
            gates = _mlstm_gates(h, w_gates_t, b_gates, heads)[:, :2 * heads]
            chunk = _tile(seq, MLSTM_CHUNK, LANES)
            gates_t = gates.reshape(batch, seq // chunk, chunk, 2 * heads).transpose(0, 3, 1, 2)
            y = _mlstm_cell(proj, gates_t[:, :, :, None, :], mlstm_norm_g[j], batch, seq, heads, dk, dv)
            xf = _mm_ws(y, mlstm_w_out, j, d, out_dtype=F32, res=xf, gate=gate1, seq=seq)

        h = _norm(xf, ffn_norm_g[i], seq, scale2, shift2, out_dtype=BF16 if i % 2 == 0 else F32)
        if i % 2 == 0:
            g = _mm_gated(h, ffn_w_gate, ffn_w_up, j, ffn_w_gate.shape[2], act="swiglu", out_dtype=BF16)
            xf = _mm_long(g, ffn_down_bf16, j, res=xf, gate=gate2, seq=seq)
        else:
            xf = _moe_layer(xf, h, gate2, seq, moe_w_router[j], moe_b_router[j],
                            moe_gate_bf16, moe_up_bf16, moe_down_bf16, j)

    out = _norm(xf, final_norm_g, seq, out_dtype=F32)
    return out.reshape(batch, seq, d)
```
